```python
import math
import jax, jax.numpy as jnp
from jax import lax
import numpy as np

D_MODEL = 1024
BATCH = 32
SEQ = 2048
DEPTH = 1
DEC_BATCH = 8
DEC_SEQ = 4096
PAST_LEN = 128

MIX_WIDTH = D_MODEL
HG_HEADS = 4
HG_KEY_DIM = 128
HG_VAL_DIM = (MIX_WIDTH // 2) // HG_HEADS
HG_FWIDTH = HG_HEADS * HG_KEY_DIM
HG_WIDTH = HG_HEADS * HG_VAL_DIM
HG_CHUNK = 64
DA_HEADS = 4
DA_QK_DIM = 64
DA_VAL_DIM = 2 * DA_QK_DIM
DA_QK_WIDTH = DA_HEADS * 2 * DA_QK_DIM
DA_WIDTH = DA_HEADS * DA_VAL_DIM
ROT_DIM = DA_QK_DIM // 4
ROPE_THETA = 500000.0
Q_BLOCK = 128
N_EXPERTS = 16
CAPACITY_FACTOR = 2
D_FF = 2816
NORM_EPS = 1e-6
IN_WIDTHS = (HG_FWIDTH, HG_FWIDTH, HG_FWIDTH, HG_WIDTH, HG_WIDTH, DA_QK_WIDTH, DA_QK_WIDTH, DA_WIDTH)
IN_WIDTH = HG_FWIDTH * 3 + HG_WIDTH * 2 + DA_QK_WIDTH * 2 + DA_WIDTH

kernel_name = "hymba_hgrn2_diffattn_ec_moe_encoder"


def rms_norm(x, w):
    xf = x.astype(jnp.float32)
    y = xf * lax.rsqrt(jnp.mean(xf * xf, axis=-1, keepdims=True) + NORM_EPS)
    return (y * w.astype(jnp.float32)).astype(x.dtype)


def split_points():
    pts, acc = [], 0
    for w in IN_WIDTHS[:-1]:
        acc += w
        pts.append(acc)
    return tuple(pts)


def gla_chunkwise(q, k, g, v):
    B, T, H, K = q.shape
    V = v.shape[-1]
    N = T // HG_CHUNK

    def to_chunks(a):
        return a.reshape(B, N, HG_CHUNK, H, a.shape[-1]).transpose(1, 0, 2, 3, 4)

    qc, kc, gc, vc = to_chunks(q), to_chunks(k), to_chunks(g), to_chunks(v)
    bc = jnp.cumsum(gc, axis=2)
    causal = jnp.tril(jnp.ones((HG_CHUNK, HG_CHUNK), dtype=bool))[None, :, :, None, None]

    def step(S, inp):
        q_, k_, b_, v_ = inp
        o_inter = jnp.einsum('bihk,bhkv->bihv', q_ * jnp.exp(b_), S)
        rel = jnp.where(causal, b_[:, :, None] - b_[:, None, :], -jnp.inf)
        att = jnp.einsum('bihk,bjhk,bijhk->bhij', q_, k_, jnp.exp(rel))
        o_intra = jnp.einsum('bhij,bjhv->bihv', att, v_)
        b_last = b_[:, -1]
        k_dec = k_ * jnp.exp(b_last[:, None] - b_)
        S_new = jnp.exp(b_last)[..., None] * S + jnp.einsum('bjhk,bjhv->bhkv', k_dec, v_)
        return S_new, o_inter + o_intra

    S0 = jnp.zeros((B, H, K, V), jnp.float32)
    _, o = lax.scan(step, S0, (qc, kc, bc, vc))
    return o.transpose(1, 0, 2, 3, 4).reshape(B, T, H, V)


def hgrn2_bidirectional(hq, hf_f, hf_b, hi, hg, lb_f, lb_b, norm_w):
    B, T, _ = hq.shape
    f32 = jnp.float32
    q = jax.nn.silu(hq.astype(f32)).reshape(B, T, HG_HEADS, HG_KEY_DIM) * (HG_KEY_DIM ** -0.5)
    v = hi.astype(f32).reshape(B, T, HG_HEADS, HG_VAL_DIM)

    def gates(fr, lb):
        f = lb + (1.0 - lb) * jax.nn.sigmoid(fr.astype(f32).reshape(B, T, HG_HEADS, HG_KEY_DIM))
        return 1.0 - f, jnp.log(f)

    k_f, g_f = gates(hf_f, lb_f)
    k_b, g_b = gates(hf_b, lb_b)
    o_fwd = gla_chunkwise(q, k_f, g_f, v)
    o_bwd = jnp.flip(gla_chunkwise(jnp.flip(q, 1), jnp.flip(k_b, 1), jnp.flip(g_b, 1), jnp.flip(v, 1)), 1)
    gate = jax.nn.silu(hg.astype(f32).reshape(B, T, HG_HEADS, HG_VAL_DIM))
    o = rms_norm(o_fwd + o_bwd, norm_w) * gate
    return o.reshape(B, T, HG_WIDTH).astype(hq.dtype)


def rope_tables(T):
    inv_freq = 1.0 / (ROPE_THETA ** (jnp.arange(0, ROT_DIM, 2, dtype=jnp.float32) / ROT_DIM))
    ang = jnp.arange(T, dtype=jnp.float32)[:, None] * inv_freq[None, :]
    return jnp.cos(ang)[:, None, None, :], jnp.sin(ang)[:, None, None, :]


def apply_partial_rope(x, cos, sin):
    half = ROT_DIM // 2
    x1, x2, rest = x[..., :half], x[..., half:ROT_DIM], x[..., ROT_DIM:]
    c, s = cos.astype(x.dtype), sin.astype(x.dtype)
    return jnp.concatenate([x1 * c - x2 * s, x2 * c + x1 * s, rest], axis=-1)


def diff_attention(aq, ak, av, layer, q_norm_w, k_norm_w, lq1, lk1, lq2, lk2, subln_w):
    B, T, _ = aq.shape
    f32 = jnp.float32
    q = rms_norm(aq.reshape(B, T, DA_HEADS, 2, DA_QK_DIM), q_norm_w)
    k = rms_norm(ak.reshape(B, T, DA_HEADS, 2, DA_QK_DIM), k_norm_w)
    cos, sin = rope_tables(T)
    q = apply_partial_rope(q, cos, sin)
    k = apply_partial_rope(k, cos, sin)
    v32 = av.reshape(B, T, DA_HEADS, DA_VAL_DIM).astype(f32)
    lam_init = 0.8 - 0.6 * math.exp(-0.3 * layer)
    lam = (jnp.exp(jnp.sum(lq1.astype(f32) * lk1.astype(f32)))
           - jnp.exp(jnp.sum(lq2.astype(f32) * lk2.astype(f32))) + lam_init)
    scale = DA_QK_DIM ** -0.5
    nb = T // Q_BLOCK
    qb = q.reshape(B, nb, Q_BLOCK, DA_HEADS, 2, DA_QK_DIM).transpose(1, 0, 2, 3, 4, 5)

    def attend(q_blk):
        s = jnp.einsum('bqhcd,bkhcd->bhcqk', q_blk, k).astype(f32) * scale
        p = jax.nn.softmax(s, axis=-1)
        pd = p[:, :, 0] - lam * p[:, :, 1]
        return jnp.einsum('bhqk,bkhv->bqhv', pd, v32)

    o = lax.map(attend, qb)
    o = o.transpose(1, 0, 2, 3, 4).reshape(B, T, DA_HEADS, DA_VAL_DIM)
    o = rms_norm(o, subln_w) * (1.0 - lam_init)
    return o.reshape(B, T, DA_WIDTH).astype(aq.dtype)


def expert_choice_moe(h, router_w, w1, w3, w2):
    B, T, D = h.shape
    n = B * T
    cap = CAPACITY_FACTOR * n // N_EXPERTS
    xt = h.reshape(n, D)
    aff = jax.nn.softmax((xt @ router_w).astype(jnp.float32), axis=-1)
    gates, idx = lax.top_k(aff.T, cap)

    def expert(args):
        w1e, w3e, w2e, ie, ge = args
        xe = jnp.take(xt, ie, axis=0)
        hid = jax.nn.silu(xe @ w1e) * (xe @ w3e)
        return (hid @ w2e).astype(jnp.float32) * ge[:, None]

    ye = lax.map(expert, (w1, w3, w2, idx, gates))
    out = jnp.zeros((n, D), jnp.float32).at[idx.reshape(-1)].add(ye.reshape(-1, D))
    return out.reshape(B, T, D).astype(h.dtype)


def hybrid_layer(x, layer, norm1_w, w_in, lb_f, lb_b, hg_norm_w, q_norm_w, k_norm_w,
                 lq1, lk1, lq2, lk2, subln_w, w_out, norm2_w, router_w, w1, w3, w2):
    xn = rms_norm(x, norm1_w)
    proj = xn @ w_in
    hq, hf_f, hf_b, hi, hg, aq, ak, av = jnp.split(proj, split_points(), axis=-1)
    o_hg = hgrn2_bidirectional(hq, hf_f, hf_b, hi, hg, lb_f, lb_b, hg_norm_w)
    o_da = diff_attention(aq, ak, av, layer, q_norm_w, k_norm_w, lq1, lk1, lq2, lk2, subln_w)
    h = x + (jnp.concatenate([o_hg, o_da], axis=-1) @ w_out).astype(x.dtype)
    return h + expert_choice_moe(rms_norm(h, norm2_w), router_w, w1, w3, w2)


def setup_inputs(seed: int = 0) -> dict:
    key = jax.random.key(seed)
    ks = jax.random.split(key, 20)

    def nrm(k, shape, scale):
        return jax.random.normal(k, shape, jnp.float32) * scale

    def gain(k, shape):
        return 1.0 + 0.02 * jax.random.normal(k, shape, jnp.float32)

    return {
        "x_prompt": nrm(ks[0], (BATCH, SEQ, D_MODEL), 1.0),
        "x_sample": nrm(ks[1], (DEC_BATCH, DEC_SEQ, D_MODEL), 1.0),
        "norm1_w": gain(ks[2], (DEPTH, D_MODEL)),
        "w_in": nrm(ks[3], (DEPTH, D_MODEL, IN_WIDTH), D_MODEL ** -0.5),
        "hg_lb_fwd": nrm(ks[4], (DEPTH + 1, HG_FWIDTH), 0.1),
        "hg_lb_bwd": nrm(ks[5], (DEPTH + 1, HG_FWIDTH), 0.1),
        "hg_norm_w": gain(ks[6], (DEPTH, HG_VAL_DIM)),
        "q_norm_w": gain(ks[7], (DEPTH, DA_QK_DIM)),
        "k_norm_w": gain(ks[8], (DEPTH, DA_QK_DIM)),
        "lambda_q1": nrm(ks[9], (DEPTH, DA_QK_DIM), 0.1),
        "lambda_k1": nrm(ks[10], (DEPTH, DA_QK_DIM), 0.1),
        "lambda_q2": nrm(ks[11], (DEPTH, DA_QK_DIM), 0.1),
        "lambda_k2": nrm(ks[12], (DEPTH, DA_QK_DIM), 0.1),
        "subln_w": gain(ks[13], (DEPTH, DA_VAL_DIM)),
        "w_out": nrm(ks[14], (DEPTH, MIX_WIDTH, D_MODEL), MIX_WIDTH ** -0.5),
        "norm2_w": gain(ks[15], (DEPTH, D_MODEL)),
        "router_w": nrm(ks[16], (DEPTH, D_MODEL, N_EXPERTS), D_MODEL ** -0.5),
        "w1": nrm(ks[17], (DEPTH, N_EXPERTS, D_MODEL, D_FF), D_MODEL ** -0.5),
        "w3": nrm(ks[18], (DEPTH, N_EXPERTS, D_MODEL, D_FF), D_MODEL ** -0.5),
        "w2": nrm(ks[19], (DEPTH, N_EXPERTS, D_FF, D_MODEL), D_FF ** -0.5),
    }


def reference(x_prompt, x_sample, norm1_w, w_in, hg_lb_fwd, hg_lb_bwd, hg_norm_w, q_norm_w, k_norm_w,
              lambda_q1, lambda_k1, lambda_q2, lambda_k2, subln_w, w_out, norm2_w, router_w, w1, w3, w2):
    lb_f_all = jnp.cumsum(jax.nn.softmax(hg_lb_fwd.astype(jnp.float32), axis=0), axis=0)
    lb_b_all = jnp.cumsum(jax.nn.softmax(hg_lb_bwd.astype(jnp.float32), axis=0), axis=0)
    y_prompt = x_prompt
    y_sample = x_sample
    for l in range(DEPTH):
        lp = (norm1_w[l], w_in[l],
              lb_f_all[l].reshape(HG_HEADS, HG_KEY_DIM), lb_b_all[l].reshape(HG_HEADS, HG_KEY_DIM),
              hg_norm_w[l], q_norm_w[l], k_norm_w[l],
              lambda_q1[l], lambda_k1[l], lambda_q2[l], lambda_k2[l], subln_w[l],
              w_out[l], norm2_w[l], router_w[l], w1[l], w3[l], w2[l])
        y_prompt = hybrid_layer(y_prompt, l, *lp)
        y_sample = hybrid_layer(y_sample, l, *lp)
    return (y_prompt, y_sample)
```

```python
import functools
import math

import jax
import jax.numpy as jnp
import numpy as np
from jax import lax
from jax.experimental import pallas as pl
from jax.experimental.pallas import tpu as pltpu

F32 = jnp.float32
BF16 = jnp.bfloat16

D_MODEL = 1024
HG_HEADS = 4
HG_KEY_DIM = 128
HG_VAL_DIM = 128
HG_WIDTH = 512
DA_HEADS = 4
DA_QK_DIM = 64
DA_VAL_DIM = 128
ROT_DIM = 16
ROPE_THETA = 500000.0
N_EXPERTS = 16
CAPACITY_FACTOR = 2
D_FF = 2816
NORM_EPS = 1e-6
GROUP_W = 512
IN_WIDTH = 8 * GROUP_W
LOG2E = 1.4426950408889634

V7X_LANES = 128
VMEM_LIMIT = 56 * 1024 * 1024


def _cparams(sem):
    return pltpu.CompilerParams(dimension_semantics=sem, vmem_limit_bytes=VMEM_LIMIT)


def _sigmoid(x):
    return 1.0 / (1.0 + jnp.exp(-x))


def _inproj_kernel(x_ref, n1_ref, w_ref, lbf_ref, lbb_ref,
                   q_ref, gf_ref, kf_ref, gb_ref, kb_ref, v_ref, gate_ref,
                   aq_ref, ak_ref, av_ref):
    x = x_ref[...]
    ms = jnp.mean(x * x, axis=-1, keepdims=True)
    y = (x * lax.rsqrt(ms + NORM_EPS) * n1_ref[...]).astype(BF16)

    def proj(j):
        return jnp.dot(y, w_ref[:, j * GROUP_W:(j + 1) * GROUP_W], preferred_element_type=F32)

    def lower_bound(lb_ref):
        p = lb_ref[...]
        m = jnp.max(p, axis=0, keepdims=True)
        e = jnp.exp(p - m)
        return e[0:1, :] / jnp.sum(e, axis=0, keepdims=True)

    hq = proj(0)
    q_ref[...] = (hq * _sigmoid(hq) * (HG_KEY_DIM ** -0.5)).astype(BF16)

    for j, lb_ref, g_ref, k_ref in ((1, lbf_ref, gf_ref, kf_ref), (2, lbb_ref, gb_ref, kb_ref)):
        lb = lower_bound(lb_ref)
        s = _sigmoid(proj(j))
        f = lb + (1.0 - lb) * s
        g_ref[...] = jnp.log2(f)
        k_ref[...] = ((1.0 - lb) * (1.0 - s)).astype(BF16)

    v_ref[...] = proj(3).astype(BF16)
    hg = proj(4)
    gate_ref[...] = (hg * _sigmoid(hg)).astype(BF16)
    aq_ref[...] = proj(5).astype(BF16)
    ak_ref[...] = proj(6).astype(BF16)
    av_ref[...] = proj(7).astype(BF16)


def _inproj(x2, norm1_w, w_in_bf, lb_f, lb_b, tm):
    n = x2.shape[0]
    row = lambda i: (i, 0)
    const = lambda i: (0, 0)
    out_dtypes = [BF16, F32, BF16, F32, BF16, BF16, BF16, BF16, BF16, BF16]
    return pl.pallas_call(
        _inproj_kernel,
        grid=(n // tm,),
        in_specs=[
            pl.BlockSpec((tm, D_MODEL), row),
            pl.BlockSpec((1, D_MODEL), const),
            pl.BlockSpec((D_MODEL, IN_WIDTH), const),
            pl.BlockSpec((2, GROUP_W), const),
            pl.BlockSpec((2, GROUP_W), const),
        ],
        out_specs=[pl.BlockSpec((tm, GROUP_W), row) for _ in out_dtypes],
        out_shape=[jax.ShapeDtypeStruct((n, GROUP_W), dt) for dt in out_dtypes],
        compiler_params=_cparams(("parallel",)),
        name="inproj",
    )(x2, norm1_w, w_in_bf, lb_f, lb_b)


def _gla_constants(chunk, reverse):
    nl = int(math.log2(chunk))
    assert 1 << nl == chunk
    tau = np.arange(chunk)[::-1] if reverse else np.arange(chunk)
    ti = tau[:, None]
    ts = tau[None, :]
    mats = []
    lvl = np.full((chunk, chunk), -1, np.int32)
    lvl[ti == ts] = 0
    for l in range(1, nl + 1):
        L, h = 1 << l, 1 << (l - 1)
        ref = (ti // L) * L + h - 1
        upper = (ti % L) >= h
        m_up = (ts > ref) & (ts <= ti)
        m_lo = (ts > ti) & (ts <= ref)
        mats.append(np.where(upper, m_up, m_lo))
        same = (ti // L) == (ts // L)
        sep = same & ((ti % L) >= h) & ((ts % L) < h)
        lvl[sep] = l
    mats.append(ts <= ti)
    mats.append(ts > ti)
    m = np.concatenate(mats, axis=0).astype(np.float32)
    m2 = np.concatenate([m, m], axis=1)
    upper_rows = np.stack([((tau % (1 << l)) >= (1 << (l - 1))) for l in range(1, nl + 1)], axis=0)
    return m2, lvl, upper_rows.astype(np.int32)[:, :, None], nl


def _gla_kernel(m_ref, lvl_ref, up_ref, q_ref, k_ref, g_ref, v_ref, *rest,
                chunk, nl, reverse, final):
    if final:
        ofwd_ref, gate_ref, nw_ref, o_ref, st_ref = rest
    else:
        o_ref, st_ref = rest
    C = chunk

    @pl.when(pl.program_id(1) == 0)
    def _():
        st_ref[...] = jnp.zeros_like(st_ref)

    g = g_ref[0]
    g_hi = g.astype(BF16)
    g_lo = (g - g_hi.astype(F32)).astype(BF16)
    args = jnp.dot(m_ref[...], jnp.concatenate([g_hi, g_lo], axis=0),
                   preferred_element_type=F32)
    e_all = jnp.exp2(args)
    lvl = lvl_ref[...]
    last = 0 if reverse else C - 1

    outs = []
    for h in range(HG_HEADS):
        sl = slice(h * HG_KEY_DIM, (h + 1) * HG_KEY_DIM)
        q = q_ref[0, :, sl].astype(F32)
        k = k_ref[0, :, sl].astype(F32)
        v = v_ref[0, :, sl]
        a = jnp.where(lvl == 0,
                      lax.dot_general(q.astype(BF16), k.astype(BF16), (((1,), (1,)), ((), ())),
                                      preferred_element_type=F32), 0.0)
        for l in range(1, nl + 1):
            e = e_all[(l - 1) * C:l * C, sl]
            x = (jnp.where(up_ref[l - 1] != 0, q, k) * e).astype(BF16)
            a_l = lax.dot_general(x, x, (((1,), (1,)), ((), ())), preferred_element_type=F32)
            a = jnp.where(lvl == l, a_l, a)
        e_q = e_all[nl * C:(nl + 1) * C, sl]
        e_k = e_all[(nl + 1) * C:(nl + 2) * C, sl]
        q_in = (q * e_q).astype(BF16)
        k_end = (k * e_k).astype(BF16)
        st = st_ref[h]
        o = jnp.dot(a.astype(BF16), v, preferred_element_type=F32)
        o = o + lax.dot_general(q_in, st.astype(BF16), (((1,), (1,)), ((), ())),
                                preferred_element_type=F32)
        st_ref[h] = st * e_q[last:last + 1, :] + lax.dot_general(
            v, k_end, (((0,), (0,)), ((), ())), preferred_element_type=F32)
        if final:
            o = o + ofwd_ref[0, :, sl]
            ms = jnp.mean(o * o, axis=-1, keepdims=True)
            o = o * lax.rsqrt(ms + NORM_EPS) * nw_ref[...]
            o = o * gate_ref[0, :, sl].astype(F32)
        outs.append(o)
    o_ref[0] = jnp.concatenate(outs, axis=-1).astype(o_ref.dtype)


def _gla(q, k, g, v, chunk, reverse, ofwd=None, gate=None, norm_w=None):
    B, T, W = q.shape
    N = T // chunk
    final = reverse
    m2, lvl, up, nl = _gla_constants(chunk, reverse)
    if reverse:
        blk = lambda b, c: (b, N - 1 - c, 0)
    else:
        blk = lambda b, c: (b, c, 0)
    c2 = lambda b, c: (0, 0)
    c3 = lambda b, c: (0, 0, 0)
    seq_spec = pl.BlockSpec((1, chunk, W), blk)
    in_specs = [
        pl.BlockSpec(m2.shape, c2),
        pl.BlockSpec(lvl.shape, c2),
        pl.BlockSpec(up.shape, c3),
        seq_spec, seq_spec, seq_spec, seq_spec,
    ]
    args = [jnp.asarray(m2, BF16), jnp.asarray(lvl), jnp.asarray(up), q, k, g, v]
    if final:
        in_specs += [seq_spec, seq_spec, pl.BlockSpec((1, HG_VAL_DIM), c2)]
        args += [ofwd, gate, norm_w]
    return pl.pallas_call(
        functools.partial(_gla_kernel, chunk=chunk, nl=nl, reverse=reverse, final=final),
        grid=(B, N),
        in_specs=in_specs,
        out_specs=seq_spec,
        out_shape=jax.ShapeDtypeStruct((B, T, W), BF16 if final else F32),
        scratch_shapes=[pltpu.VMEM((HG_HEADS, HG_VAL_DIM, HG_KEY_DIM), F32)],
        compiler_params=_cparams(("parallel", "arbitrary")),
        name="gla_bwd" if reverse else "gla_fwd",
    )(*args)


def _rope_tables(T):
    half = ROT_DIM // 2
    inv_freq = 1.0 / (ROPE_THETA ** (jnp.arange(0, ROT_DIM, 2, dtype=F32) / ROT_DIM))
    ang = jnp.arange(T, dtype=F32)[:, None] * inv_freq[None, :]
    cos, sin = jnp.cos(ang), jnp.sin(ang)
    ones = jnp.ones((T, DA_QK_DIM - ROT_DIM), F32)
    zeros = jnp.zeros((T, DA_QK_DIM - ROT_DIM), F32)
    zh = jnp.zeros((T, half), F32)
    c = jnp.concatenate([cos, cos, ones], axis=1)
    sa = jnp.concatenate([-sin, zh, zeros], axis=1)
    sb = jnp.concatenate([zh, sin, zeros], axis=1)
    return tuple(jnp.tile(t, (1, 2)) for t in (c, sa, sb))


def _qk_prep(x, w, seg, c, sa, sb):
    ss = x * x
    ss_hi = ss.astype(BF16)
    ss_lo = (ss - ss_hi.astype(F32)).astype(BF16)
    tot = (jnp.dot(ss_hi, seg, preferred_element_type=F32)
           + jnp.dot(ss_lo, seg, preferred_element_type=F32))
    xn = x * lax.rsqrt(tot * (1.0 / DA_QK_DIM) + NORM_EPS) * w
    lanes = xn.shape[-1]
    return (xn * c + pltpu.roll(xn, lanes - ROT_DIM // 2, 1) * sa
            + pltpu.roll(xn, ROT_DIM // 2, 1) * sb)


def _attn_kernel(aq_ref, ak_ref, av_ref, cq_ref, saq_ref, sbq_ref, ck_ref, sak_ref, sbk_ref,
                 qnw_ref, knw_ref, seg_ref, lq1_ref, lk1_ref, lq2_ref, lk2_ref, sub_ref,
                 o_ref, kp_ref, v2_ref, *, tq, lam_init):
    @pl.when(pl.program_id(2) == 0)
    def _():
        kp = _qk_prep(ak_ref[0].astype(F32), knw_ref[...], seg_ref[...],
                      ck_ref[...], sak_ref[...], sbk_ref[...])
        kp_ref[...] = kp.astype(BF16)
        v2_ref[:, :DA_VAL_DIM] = av_ref[0]
        v2_ref[:, DA_VAL_DIM:] = jnp.ones((v2_ref.shape[0], DA_VAL_DIM), BF16)

    q = _qk_prep(aq_ref[0].astype(F32), qnw_ref[...], seg_ref[...],
                 cq_ref[...], saq_ref[...], sbq_ref[...])
    q = q * (DA_QK_DIM ** -0.5 * LOG2E)
    lane = lax.broadcasted_iota(jnp.int32, q.shape, 1)
    qq = jnp.concatenate([jnp.where(lane < DA_QK_DIM, q, 0.0),
                          jnp.where(lane >= DA_QK_DIM, q, 0.0)], axis=0).astype(BF16)
    s = lax.dot_general(qq, kp_ref[...], (((1,), (1,)), ((), ())), preferred_element_type=F32)
    m = jnp.max(s, axis=-1, keepdims=True)
    p = jnp.exp2(s - m).astype(BF16)
    ov = jnp.dot(p, v2_ref[...], preferred_element_type=F32)
    oc = ov[:, :DA_VAL_DIM] / ov[:, DA_VAL_DIM:DA_VAL_DIM + 1]
    lam = (jnp.exp(jnp.sum(lq1_ref[...] * lk1_ref[...], axis=-1, keepdims=True))
           - jnp.exp(jnp.sum(lq2_ref[...] * lk2_ref[...], axis=-1, keepdims=True)) + lam_init)
    o = oc[:tq] - lam * oc[tq:]
    ms = jnp.mean(o * o, axis=-1, keepdims=True)
    o = o * lax.rsqrt(ms + NORM_EPS) * sub_ref[...] * (1.0 - lam_init)
    o_ref[0] = o.astype(o_ref.dtype)


def _attention(aq, ak, av, q_norm_w, k_norm_w, lq1, lk1, lq2, lk2, subln_w, layer, tq):
    B, T, W = aq.shape
    c, sa, sb = _rope_tables(T)
    seg = jnp.asarray(np.kron(np.eye(2), np.ones((DA_QK_DIM, DA_QK_DIM))), BF16)
    qnw = jnp.tile(q_norm_w.reshape(1, DA_QK_DIM), (1, 2))
    knw = jnp.tile(k_norm_w.reshape(1, DA_QK_DIM), (1, 2))
    lam_init = 0.8 - 0.6 * math.exp(-0.3 * layer)
    qblk = pl.BlockSpec((1, tq, DA_VAL_DIM), lambda b, h, i: (b, i, h))
    kblk = pl.BlockSpec((1, T, DA_VAL_DIM), lambda b, h, i: (b, 0, h))
    tq_spec = pl.BlockSpec((tq, V7X_LANES), lambda b, h, i: (i, 0))
    tk_spec = pl.BlockSpec((T, V7X_LANES), lambda b, h, i: (0, 0))
    row128 = pl.BlockSpec((1, V7X_LANES), lambda b, h, i: (0, 0))
    row64 = pl.BlockSpec((1, DA_QK_DIM), lambda b, h, i: (0, 0))
    return pl.pallas_call(
        functools.partial(_attn_kernel, tq=tq, lam_init=lam_init),
        grid=(B, DA_HEADS, T // tq),
        in_specs=[qblk, kblk, kblk, tq_spec, tq_spec, tq_spec, tk_spec, tk_spec, tk_spec,
                  row128, row128, pl.BlockSpec((V7X_LANES, V7X_LANES), lambda b, h, i: (0, 0)),
                  row64, row64, row64, row64, row128],
        out_specs=qblk,
        out_shape=jax.ShapeDtypeStruct((B, T, W), BF16),
        scratch_shapes=[pltpu.VMEM((T, DA_VAL_DIM), BF16), pltpu.VMEM((T, 2 * DA_VAL_DIM), BF16)],
        compiler_params=_cparams(("parallel", "parallel", "arbitrary")),
        name="diff_attn",
    )(aq, ak, av, c, sa, sb, c, sa, sb, qnw, knw, seg,
      lq1.reshape(1, -1), lk1.reshape(1, -1), lq2.reshape(1, -1), lk2.reshape(1, -1),
      subln_w.reshape(1, -1))


def _outproj_kernel(ohg_ref, oda_ref, x_ref, wo_ref, n2_ref, rwh_ref, rwl_ref,
                    h_ref, xn_ref, aff_ref):
    mix = (jnp.dot(ohg_ref[...], wo_ref[:HG_WIDTH, :], preferred_element_type=F32)
           + jnp.dot(oda_ref[...], wo_ref[HG_WIDTH:, :], preferred_element_type=F32))
    h = x_ref[...] + mix
    h_ref[...] = h
    ms = jnp.mean(h * h, axis=-1, keepdims=True)
    xn = h * lax.rsqrt(ms + NORM_EPS) * n2_ref[...]
    xn_hi = xn.astype(BF16)
    xn_ref[...] = xn_hi
    xn_lo = (xn - xn_hi.astype(F32)).astype(BF16)
    logits = (jnp.dot(xn_hi, rwh_ref[...], preferred_element_type=F32)
              + jnp.dot(xn_lo, rwh_ref[...], preferred_element_type=F32)
              + jnp.dot(xn_hi, rwl_ref[...], preferred_element_type=F32))
    lane = lax.broadcasted_iota(jnp.int32, logits.shape, 1)
    logits = jnp.where(lane < N_EXPERTS, logits, -jnp.inf)
    m = jnp.max(logits, axis=-1, keepdims=True)
    e = jnp.exp(logits - m)
    aff_ref[...] = e / jnp.sum(e, axis=-1, keepdims=True)


def _outproj(ohg, oda, x2, w_out_bf, norm2_w, rw_hi, rw_lo, tm):
    n = x2.shape[0]
    row = lambda i: (i, 0)
    const = lambda i: (0, 0)
    return pl.pallas_call(
        _outproj_kernel,
        grid=(n // tm,),
        in_specs=[
            pl.BlockSpec((tm, HG_WIDTH), row),
            pl.BlockSpec((tm, HG_WIDTH), row),
            pl.BlockSpec((tm, D_MODEL), row),
            pl.BlockSpec((D_MODEL, D_MODEL), const),
            pl.BlockSpec((1, D_MODEL), const),
            pl.BlockSpec((D_MODEL, V7X_LANES), const),
            pl.BlockSpec((D_MODEL, V7X_LANES), const),
        ],
        out_specs=[pl.BlockSpec((tm, D_MODEL), row), pl.BlockSpec((tm, D_MODEL), row),
                   pl.BlockSpec((tm, V7X_LANES), row)],
        out_shape=[jax.ShapeDtypeStruct((n, D_MODEL), F32), jax.ShapeDtypeStruct((n, D_MODEL), BF16),
                   jax.ShapeDtypeStruct((n, V7X_LANES), F32)],
        compiler_params=_cparams(("parallel",)),
        name="outproj",
    )(ohg, oda, x2, w_out_bf, norm2_w, rw_hi, rw_lo)


def _select_kernel(aff_ref, tri_ref, low_ref, slot_ref, gate_ref, idx_ref, toff_ref, *, cap):
    aff = aff_ref[0]
    R = aff.shape[0]
    bits = pltpu.bitcast(aff, jnp.int32)

    def count(mask):
        c = jnp.sum(mask.astype(F32), axis=0, keepdims=True)
        return jnp.sum(c, axis=1, keepdims=True)

    def search(i, t):
        cand = t | (1 << (30 - i))
        return jnp.where(count(bits >= cand) >= cap, cand, t)

    thr = lax.fori_loop(0, 31, search, jnp.zeros((1, 1), jnp.int32))
    gt = bits > thr
    eq = bits == thr
    need = cap - count(gt)

    def prefix(mask):
        x = jnp.where(mask, 1.0, 0.0).astype(BF16)
        pin = jnp.dot(x, tri_ref[...], preferred_element_type=F32)
        off = jnp.dot(low_ref[...], pin.astype(BF16), preferred_element_type=F32)[:, V7X_LANES - 1:]
        return off + pin - x.astype(F32), off, pin

    pe, _, _ = prefix(eq)
    sel = gt | (eq & (pe < need))
    slot, off, pin = prefix(sel)
    slot_ref[0] = jnp.where(sel, slot, -1.0).astype(jnp.int32)
    gate_ref[0] = jnp.where(sel, aff, 0.0)

    s_row = lax.broadcasted_iota(jnp.int32, (1, cap), 1).astype(F32)
    row_incl = off + pin[:, V7X_LANES - 1:]
    r_of_s = jnp.sum((row_incl <= s_row).astype(F32), axis=0, keepdims=True)
    onehot = lax.broadcasted_iota(jnp.int32, (R, cap), 0).astype(F32) == r_of_s
    base = jnp.sum(jnp.where(onehot, off, 0.0), axis=0, keepdims=True)
    prow = lax.dot_general(pin.astype(BF16), jnp.where(onehot, 1.0, 0.0).astype(BF16),
                           (((0,), (0,)), ((), ())), preferred_element_type=F32)
    l_of_s = jnp.sum((prow <= s_row - base).astype(F32), axis=0, keepdims=True)
    idx_ref[0] = (r_of_s * V7X_LANES + l_of_s).astype(jnp.int32)
    toff_ref[0] = off.astype(jnp.int32)


def _select(aff_e, cap):
    E, R, L = aff_e.shape
    tri = jnp.asarray(np.triu(np.ones((L, L), np.float32)), BF16)
    low = jnp.asarray(np.tril(np.ones((R, R), np.float32), -1), BF16)
    blk = pl.BlockSpec((1, R, L), lambda e: (e, 0, 0))
    return pl.pallas_call(
        functools.partial(_select_kernel, cap=cap),
        grid=(E,),
        in_specs=[blk, pl.BlockSpec((L, L), lambda e: (0, 0)), pl.BlockSpec((R, R), lambda e: (0, 0))],
        out_specs=[blk, blk, pl.BlockSpec((1, 1, cap), lambda e: (e, 0, 0)),
                   pl.BlockSpec((1, R, 1), lambda e: (e, 0, 0))],
        out_shape=[jax.ShapeDtypeStruct((E, R, L), jnp.int32), jax.ShapeDtypeStruct((E, R, L), F32),
                   jax.ShapeDtypeStruct((E, 1, cap), jnp.int32), jax.ShapeDtypeStruct((E, R, 1), jnp.int32)],
        compiler_params=_cparams(("parallel",)),
        name="select",
    )(aff_e, tri, low)


def _gather_kernel(idx_ref, x_hbm, o_ref, sem, *, rows):
    def row_copy(r):
        return pltpu.make_async_copy(x_hbm.at[idx_ref[0, 0, r]], o_ref.at[r], sem)

    def issue(r, carry):
        row_copy(r).start()
        return carry

    def drain(r, carry):
        row_copy(r).wait()
        return carry

    lax.fori_loop(0, rows, issue, 0)
    lax.fori_loop(0, rows, drain, 0)


def _gather(x3, idx_blocks, rows):
    nblk = idx_blocks.shape[0]
    return pl.pallas_call(
        functools.partial(_gather_kernel, rows=rows),
        grid=(nblk,),
        in_specs=[pl.BlockSpec((1, 1, rows), lambda i: (i, 0, 0), memory_space=pltpu.SMEM),
                  pl.BlockSpec(memory_space=pl.ANY)],
        out_specs=pl.BlockSpec((rows,) + x3.shape[1:], lambda i: (i, 0, 0)),
        out_shape=jax.ShapeDtypeStruct((nblk * rows,) + x3.shape[1:], x3.dtype),
        scratch_shapes=[pltpu.SemaphoreType.DMA],
        compiler_params=_cparams(("arbitrary",)),
        name="moe_gather",
    )(idx_blocks, x3)


def _ffn_kernel(x_ref, w1_ref, w3_ref, w2_ref, o_ref, acc_ref):
    f = pl.program_id(2)
    x = x_ref[...]
    a = jnp.dot(x, w1_ref[0], preferred_element_type=F32)
    b = jnp.dot(x, w3_ref[0], preferred_element_type=F32)
    hid = (a * _sigmoid(a) * b).astype(BF16)
    y = jnp.dot(hid, w2_ref[0], preferred_element_type=F32)

    @pl.when(f == 0)
    def _():
        acc_ref[...] = y

    @pl.when(f > 0)
    def _():
        acc_ref[...] += y

    @pl.when(f == pl.num_programs(2) - 1)
    def _():
        o_ref[...] = acc_ref[...].astype(o_ref.dtype)


def _ffn(xe, w1, w3, w2, cap, tm, tf):
    E = w1.shape[0]
    tiles = cap // tm
    nf = D_FF // tf
    return pl.pallas_call(
        _ffn_kernel,
        grid=(E, tiles, nf),
        in_specs=[
            pl.BlockSpec((tm, D_MODEL), lambda e, t, f: (e * tiles + t, 0)),
            pl.BlockSpec((1, D_MODEL, tf), lambda e, t, f: (e, 0, f)),
            pl.BlockSpec((1, D_MODEL, tf), lambda e, t, f: (e, 0, f)),
            pl.BlockSpec((1, tf, D_MODEL), lambda e, t, f: (e, f, 0)),
        ],
        out_specs=pl.BlockSpec((tm, D_MODEL), lambda e, t, f: (e * tiles + t, 0)),
        out_shape=jax.ShapeDtypeStruct((E * cap, D_MODEL), BF16),
        scratch_shapes=[pltpu.VMEM((tm, D_MODEL), F32)],
        compiler_params=_cparams(("parallel", "parallel", "arbitrary")),
        name="moe_ffn",
    )(xe, w1, w3, w2)


COMBINE_ALIGN = 16


def _combine_kernel(start_ref, nwin_ref, h_ref, slot_ref, gate_ref, ye_hbm, o_ref,
                    buf_ref, xbuf_ref, sem, xsem, *, cap, win):
    j = pl.program_id(0)
    nt = pl.num_programs(0)
    cur = j % 2
    E = N_EXPERTS

    def window_copy(tile, e, w, dst, s):
        first = start_ref[e, tile] + w * win
        src = jnp.minimum(first, cap - win)
        src = pl.multiple_of(src, COMBINE_ALIGN)
        return pltpu.make_async_copy(ye_hbm.at[pl.ds(e * cap + src, win)], dst, s), first, src

    def fetch(tile, slot):
        for e in range(E):
            window_copy(tile, e, 0, buf_ref.at[slot, e], sem.at[slot, e])[0].start()

    @pl.when(j == 0)
    def _():
        fetch(0, 0)

    @pl.when(j + 1 < nt)
    def _():
        fetch(j + 1, 1 - cur)

    tj = h_ref.shape[0]
    lane = lax.broadcasted_iota(jnp.int32, (tj, win), 1)
    acc = h_ref[...]
    for e in range(E):
        s_col = slot_ref[:, e:e + 1]
        g_col = gate_ref[:, e:e + 1]

        def pick(first, src, rows):
            hit = (s_col - src == lane) & (s_col >= first) & (s_col < first + win)
            return g_col * jnp.dot(jnp.where(hit, 1.0, 0.0).astype(BF16), rows,
                                   preferred_element_type=F32)

        cp, first, src = window_copy(j, e, 0, buf_ref.at[cur, e], sem.at[cur, e])
        cp.wait()
        acc = acc + pick(first, src, buf_ref[cur, e])

        def extra(w, acc):
            cp, first, src = window_copy(j, e, w, xbuf_ref, xsem)
            cp.start()
            cp.wait()
            return acc + pick(first, src, xbuf_ref[...])

        acc = lax.fori_loop(1, nwin_ref[e, j], extra, acc)
    o_ref[...] = acc


def _combine(h2, slot_t, gate_t, ye, starts, nwin, cap, tj, win):
    n = h2.shape[0]
    E = N_EXPERTS
    grid_spec = pltpu.PrefetchScalarGridSpec(
        num_scalar_prefetch=2,
        grid=(n // tj,),
        in_specs=[
            pl.BlockSpec((tj, D_MODEL), lambda j, s, w: (j, 0)),
            pl.BlockSpec((tj, E), lambda j, s, w: (j, 0)),
            pl.BlockSpec((tj, E), lambda j, s, w: (j, 0)),
            pl.BlockSpec(memory_space=pl.ANY),
        ],
        out_specs=pl.BlockSpec((tj, D_MODEL), lambda j, s, w: (j, 0)),
        scratch_shapes=[
            pltpu.VMEM((2, E, win, D_MODEL), BF16),
            pltpu.VMEM((win, D_MODEL), BF16),
            pltpu.SemaphoreType.DMA((2, E)),
            pltpu.SemaphoreType.DMA,
        ],
    )
    return pl.pallas_call(
        functools.partial(_combine_kernel, cap=cap, win=win),
        grid_spec=grid_spec,
        out_shape=jax.ShapeDtypeStruct((n, D_MODEL), F32),
        compiler_params=_cparams(("arbitrary",)),
        name="moe_combine",
    )(starts, nwin, h2, slot_t, gate_t, ye)


TM_INPROJ = 256
TM_OUTPROJ = 256
GLA_CHUNK = 128
ATTN_TQ = 256
GATHER_ROWS = 256
FFN_TM = 512
FFN_TF = 1408
COMBINE_TJ = 512
COMBINE_WIN = 144


def _moe(h2, xn2, aff, w1, w3, w2):
    n = h2.shape[0]
    E = N_EXPERTS
    cap = CAPACITY_FACTOR * n // E
    R = n // V7X_LANES
    aff_e = aff[:, :E].T.reshape(E, R, V7X_LANES)
    slot, gate, idx, toff = _select(aff_e, cap)

    rows = min(GATHER_ROWS, cap)
    x3 = xn2.reshape(n, D_MODEL // V7X_LANES, V7X_LANES)
    xe = _gather(x3, idx.reshape(E * cap // rows, 1, rows), rows).reshape(E * cap, D_MODEL)
    ye = _ffn(xe, w1, w3, w2, cap, min(FFN_TM, cap), FFN_TF)

    tj = min(COMBINE_TJ, n)
    win = min(COMBINE_WIN, cap)
    per = tj // V7X_LANES
    off = toff.reshape(E, R)[:, ::per]
    cnt = jnp.concatenate([off[:, 1:], jnp.full((E, 1), cap, jnp.int32)], axis=1) - off
    starts = (off // COMBINE_ALIGN) * COMBINE_ALIGN
    nwin = jnp.maximum(1, (off - starts + cnt + win - 1) // win)
    slot_t = slot.reshape(E, n).T
    gate_t = gate.reshape(E, n).T
    return _combine(h2, slot_t, gate_t, ye, starts, nwin, cap, tj, win)


def _layer(x, layer, norm1_w, w_in_bf, lb_f, lb_b, hg_norm_w, q_norm_w, k_norm_w,
           lq1, lk1, lq2, lk2, subln_w, w_out_bf, norm2_w, rw_hi, rw_lo, w1, w3, w2):
    B, T, D = x.shape
    n = B * T
    x2 = x.reshape(n, D)
    q, gf, kf, gb, kb, v, gate, aq, ak, av = [
        o.reshape(B, T, GROUP_W)
        for o in _inproj(x2, norm1_w, w_in_bf, lb_f, lb_b, min(TM_INPROJ, n))]
    chunk = min(GLA_CHUNK, T)
    o_fwd = _gla(q, kf, gf, v, chunk, False)
    o_hg = _gla(q, kb, gb, v, chunk, True, ofwd=o_fwd, gate=gate, norm_w=hg_norm_w)
    o_da = _attention(aq, ak, av, q_norm_w, k_norm_w, lq1, lk1, lq2, lk2, subln_w, layer,
                      min(ATTN_TQ, T))
    h2, xn2, aff = _outproj(o_hg.reshape(n, GROUP_W), o_da.reshape(n, GROUP_W), x2, w_out_bf,
                            norm2_w, rw_hi, rw_lo, min(TM_OUTPROJ, n))
    return _moe(h2, xn2, aff, w1, w3, w2).reshape(B, T, D)


def kernel(x_prompt, x_sample, norm1_w, w_in, hg_lb_fwd, hg_lb_bwd, hg_norm_w, q_norm_w, k_norm_w,
           lambda_q1, lambda_k1, lambda_q2, lambda_k2, subln_w, w_out, norm2_w, router_w, w1, w3, w2):
    depth = w_in.shape[0]
    assert depth == 1 and hg_lb_fwd.shape[0] == 2
    y_prompt, y_sample = x_prompt, x_sample
    for l in range(depth):
        rw = jnp.pad(router_w[l], ((0, 0), (0, V7X_LANES - N_EXPERTS)))
        rw_hi = rw.astype(BF16)
        rw_lo = (rw - rw_hi.astype(F32)).astype(BF16)
        lp = (l, norm1_w[l:l + 1], w_in[l].astype(BF16), hg_lb_fwd, hg_lb_bwd, hg_norm_w[l:l + 1],
              q_norm_w[l], k_norm_w[l], lambda_q1[l], lambda_k1[l], lambda_q2[l], lambda_k2[l],
              subln_w[l], w_out[l].astype(BF16), norm2_w[l:l + 1], rw_hi, rw_lo,
              w1[l].astype(BF16), w3[l].astype(BF16), w2[l].astype(BF16))
        y_prompt = _layer(y_prompt, *lp)
        y_sample = _layer(y_sample, *lp)
    return (y_prompt, y_sample)
```

```python
import functools
import math

import jax
import jax.numpy as jnp
import numpy as np
from jax import lax
from jax.experimental import pallas as pl
from jax.experimental.pallas import tpu as pltpu

F32 = jnp.float32
BF16 = jnp.bfloat16

D_MODEL = 1024
HG_HEADS = 4
HG_KEY_DIM = 128
HG_VAL_DIM = 128
HG_WIDTH = 512
DA_HEADS = 4
DA_QK_DIM = 64
DA_VAL_DIM = 128
ROT_DIM = 16
ROPE_THETA = 500000.0
N_EXPERTS = 16
CAPACITY_FACTOR = 2
D_FF = 2816
NORM_EPS = 1e-6
GROUP_W = 512
IN_WIDTH = 8 * GROUP_W
LOG2E = 1.4426950408889634

V7X_LANES = 128
VMEM_LIMIT = 56 * 1024 * 1024


def _cparams(sem):
    return pltpu.CompilerParams(dimension_semantics=sem, vmem_limit_bytes=VMEM_LIMIT)


def _sigmoid(x):
    return 1.0 / (1.0 + jnp.exp(-x))


def _rope_tables(T):
    half = ROT_DIM // 2
    inv_freq = 1.0 / (ROPE_THETA ** (jnp.arange(0, ROT_DIM, 2, dtype=F32) / ROT_DIM))
    ang = jnp.arange(T, dtype=F32)[:, None] * inv_freq[None, :]
    cos, sin = jnp.cos(ang), jnp.sin(ang)
    ones = jnp.ones((T, DA_QK_DIM - ROT_DIM), F32)
    zeros = jnp.zeros((T, DA_QK_DIM - ROT_DIM), F32)
    zh = jnp.zeros((T, half), F32)
    c = jnp.concatenate([cos, cos, ones], axis=1)
    sa = jnp.concatenate([-sin, zh, zeros], axis=1)
    sb = jnp.concatenate([zh, sin, zeros], axis=1)
    return tuple(jnp.tile(t, (1, 2)) for t in (c, sa, sb))


def _qk_prep(x, w, c, sa, sb):
    heads = []
    for h in range(DA_HEADS):
        xh = x[:, h * V7X_LANES:(h + 1) * V7X_LANES]
        first = lax.broadcasted_iota(jnp.int32, xh.shape, 1) < DA_QK_DIM
        ss = xh * xh
        s0 = jnp.sum(jnp.where(first, ss, 0.0), axis=-1, keepdims=True)
        s1 = jnp.sum(jnp.where(first, 0.0, ss), axis=-1, keepdims=True)
        r = lax.rsqrt(jnp.where(first, s0, s1) * (1.0 / DA_QK_DIM) + NORM_EPS)
        heads.append(xh * r)
    xn = jnp.concatenate(heads, axis=-1) * w
    lanes = xn.shape[-1]
    return (xn * c + pltpu.roll(xn, lanes - ROT_DIM // 2, 1) * sa
            + pltpu.roll(xn, ROT_DIM // 2, 1) * sb)


def _inproj_kernel(x_ref, n1_ref, w_ref, lbf_ref, lbb_ref, qnw_ref, knw_ref, c_ref, sa_ref, sb_ref,
                   q_ref, gf_ref, kf_ref, gb_ref, kb_ref, v_ref, gate_ref,
                   aq_ref, ak_ref, av_ref):
    x = x_ref[...]
    ms = jnp.mean(x * x, axis=-1, keepdims=True)
    y = (x * lax.rsqrt(ms + NORM_EPS) * n1_ref[...]).astype(BF16)

    def proj(j):
        return jnp.dot(y, w_ref[:, j * GROUP_W:(j + 1) * GROUP_W], preferred_element_type=F32)

    def lower_bound(lb_ref):
        p = lb_ref[...]
        m = jnp.max(p, axis=0, keepdims=True)
        e = jnp.exp(p - m)
        return e[0:1, :] / jnp.sum(e, axis=0, keepdims=True)

    hq = proj(0)
    q_ref[...] = (hq * _sigmoid(hq) * (HG_KEY_DIM ** -0.5)).astype(BF16)

    for j, lb_ref, g_ref, k_ref in ((1, lbf_ref, gf_ref, kf_ref), (2, lbb_ref, gb_ref, kb_ref)):
        lb = lower_bound(lb_ref)
        s = _sigmoid(proj(j))
        f = lb + (1.0 - lb) * s
        g_ref[...] = jnp.log2(f)
        k_ref[...] = ((1.0 - lb) * (1.0 - s)).astype(BF16)

    v_ref[...] = proj(3).astype(BF16)
    hg = proj(4)
    gate_ref[...] = (hg * _sigmoid(hg)).astype(BF16)
    rope = (c_ref[...], sa_ref[...], sb_ref[...])
    aq = _qk_prep(proj(5), qnw_ref[...], *rope)
    aq_ref[...] = (aq * (DA_QK_DIM ** -0.5 * LOG2E)).astype(BF16)
    ak_ref[...] = _qk_prep(proj(6), knw_ref[...], *rope).astype(BF16)
    av_ref[...] = proj(7).astype(BF16)


def _inproj(x2, norm1_w, w_in_bf, lb_f, lb_b, q_norm_w, k_norm_w, T, tm):
    n = x2.shape[0]
    row = lambda i: (i, 0)
    const = lambda i: (0, 0)
    pos = lambda i: (i % (T // tm), 0)
    rope = [jnp.tile(t, (1, DA_HEADS)) for t in _rope_tables(T)]
    qnw = jnp.tile(q_norm_w.reshape(1, DA_QK_DIM), (1, 2 * DA_HEADS))
    knw = jnp.tile(k_norm_w.reshape(1, DA_QK_DIM), (1, 2 * DA_HEADS))
    out_dtypes = [BF16, F32, BF16, F32, BF16, BF16, BF16, BF16, BF16, BF16]
    return pl.pallas_call(
        _inproj_kernel,
        grid=(n // tm,),
        in_specs=[
            pl.BlockSpec((tm, D_MODEL), row),
            pl.BlockSpec((1, D_MODEL), const),
            pl.BlockSpec((D_MODEL, IN_WIDTH), const),
            pl.BlockSpec((2, GROUP_W), const),
            pl.BlockSpec((2, GROUP_W), const),
            pl.BlockSpec((1, GROUP_W), const),
            pl.BlockSpec((1, GROUP_W), const),
            pl.BlockSpec((tm, GROUP_W), pos),
            pl.BlockSpec((tm, GROUP_W), pos),
            pl.BlockSpec((tm, GROUP_W), pos),
        ],
        out_specs=[pl.BlockSpec((tm, GROUP_W), row) for _ in out_dtypes],
        out_shape=[jax.ShapeDtypeStruct((n, GROUP_W), dt) for dt in out_dtypes],
        compiler_params=_cparams(("parallel",)),
        name="inproj",
    )(x2, norm1_w, w_in_bf, lb_f, lb_b, qnw, knw, *rope)


def _gla_constants(chunk, reverse):
    nl = int(math.log2(chunk))
    assert 1 << nl == chunk
    tau = np.arange(chunk)[::-1] if reverse else np.arange(chunk)
    ti = tau[:, None]
    ts = tau[None, :]
    mats = []
    lvl = np.full((chunk, chunk), -1, np.int32)
    lvl[ti == ts] = 0
    for l in range(1, nl + 1):
        L, h = 1 << l, 1 << (l - 1)
        ref = (ti // L) * L + h - 1
        upper = (ti % L) >= h
        m_up = (ts > ref) & (ts <= ti)
        m_lo = (ts > ti) & (ts <= ref)
        mats.append(np.where(upper, m_up, m_lo))
        same = (ti // L) == (ts // L)
        sep = same & ((ti % L) >= h) & ((ts % L) < h)
        lvl[sep] = l
    mats.append(ts <= ti)
    mats.append(ts > ti)
    m = np.concatenate(mats, axis=0).astype(np.float32)
    m2 = np.concatenate([m, m], axis=1)
    upper_rows = np.stack([((tau % (1 << l)) >= (1 << (l - 1))) for l in range(1, nl + 1)], axis=0)
    return m2, lvl, upper_rows.astype(np.int32)[:, :, None], nl


def _gla_kernel(m_ref, lvl_ref, up_ref, q_ref, k_ref, g_ref, v_ref, *rest,
                chunk, nl, reverse, final):
    if final:
        ofwd_ref, gate_ref, nw_ref, o_ref, st_ref = rest
    else:
        o_ref, st_ref = rest
    C = chunk
    nb = q_ref.shape[0]

    @pl.when(pl.program_id(1) == 0)
    def _():
        st_ref[...] = jnp.zeros_like(st_ref)

    lvl = lvl_ref[...]
    level_mask = [lvl == l for l in range(1, nl + 1)]
    upper = [up_ref[l] != 0 for l in range(nl)]
    last = 0 if reverse else C - 1

    parts = []
    for b in range(nb):
        g = g_ref[b]
        g_hi = g.astype(BF16)
        parts += [g_hi, (g - g_hi.astype(F32)).astype(BF16)]
    e_all = []
    for b in range(nb):
        args = jnp.dot(m_ref[...], jnp.concatenate(parts[2 * b:2 * b + 2], axis=0),
                       preferred_element_type=F32)
        e_all.append(jnp.exp2(args))

    W2 = 2 * HG_KEY_DIM
    lane2 = lax.broadcasted_iota(jnp.int32, (C, W2), 1)
    first = lane2 < HG_KEY_DIM
    blk_diag = (lax.broadcasted_iota(jnp.int32, (W2, W2), 0) < HG_KEY_DIM) == (
        lax.broadcasted_iota(jnp.int32, (W2, W2), 1) < HG_KEY_DIM)
    level_mask2 = [jnp.concatenate([m, m], axis=1) for m in level_mask]

    def per_head_rows(x):
        z = jnp.zeros_like(x)
        return jnp.concatenate([jnp.where(first, x, z), jnp.where(first, z, x)], axis=0)

    for b in range(nb):
        outs = []
        for p in range(HG_HEADS // 2):
            sl = slice(p * W2, (p + 1) * W2)
            q = q_ref[b, :, sl].astype(F32)
            k = k_ref[b, :, sl].astype(F32)
            v = v_ref[b, :, sl]
            a = jnp.zeros((C, 2 * C), F32)
            for l in range(1, nl + 1):
                e = e_all[b][(l - 1) * C:l * C, sl]
                x = (jnp.where(upper[l - 1], q, k) * e).astype(BF16)
                a_l = lax.dot_general(x, per_head_rows(x), (((1,), (1,)), ((), ())),
                                      preferred_element_type=F32)
                a = jnp.where(level_mask2[l - 1], a_l, a)
            qk = q * k
            diag = jnp.where(first, jnp.sum(jnp.where(first, qk, 0.0), axis=-1, keepdims=True),
                             jnp.sum(jnp.where(first, 0.0, qk), axis=-1, keepdims=True))
            e_q = e_all[b][nl * C:(nl + 1) * C, sl]
            e_k = e_all[b][(nl + 1) * C:(nl + 2) * C, sl]
            q_in = (q * e_q).astype(BF16)
            k_end = (k * e_k).astype(BF16)
            st = st_ref[b, p]
            o = jnp.dot(a.astype(BF16), per_head_rows(v), preferred_element_type=F32)
            o = o + diag * v.astype(F32)
            o = o + lax.dot_general(q_in, st.astype(BF16), (((1,), (1,)), ((), ())),
                                    preferred_element_type=F32)
            upd = lax.dot_general(v, k_end, (((0,), (0,)), ((), ())), preferred_element_type=F32)
            st_ref[b, p] = st * e_q[last:last + 1, :] + jnp.where(blk_diag, upd, 0.0)
            if final:
                o = o + ofwd_ref[b, :, sl]
                oo = o * o
                ms = jnp.where(first, jnp.sum(jnp.where(first, oo, 0.0), axis=-1, keepdims=True),
                               jnp.sum(jnp.where(first, 0.0, oo), axis=-1, keepdims=True))
                o = o * lax.rsqrt(ms * (1.0 / HG_VAL_DIM) + NORM_EPS) * nw_ref[...]
                o = o * gate_ref[b, :, sl].astype(F32)
            outs.append(o)
        o_ref[b] = jnp.concatenate(outs, axis=-1).astype(o_ref.dtype)


def _gla(q, k, g, v, chunk, nb, reverse, ofwd=None, gate=None, norm_w=None):
    B, T, W = q.shape
    N = T // chunk
    final = reverse
    m2, lvl, up, nl = _gla_constants(chunk, reverse)
    if reverse:
        blk = lambda b, c: (b, N - 1 - c, 0)
    else:
        blk = lambda b, c: (b, c, 0)
    c2 = lambda b, c: (0, 0)
    c3 = lambda b, c: (0, 0, 0)
    seq_spec = pl.BlockSpec((nb, chunk, W), blk)
    in_specs = [
        pl.BlockSpec(m2.shape, c2),
        pl.BlockSpec(lvl.shape, c2),
        pl.BlockSpec(up.shape, c3),
        seq_spec, seq_spec, seq_spec, seq_spec,
    ]
    args = [jnp.asarray(m2, BF16), jnp.asarray(lvl), jnp.asarray(up), q, k, g, v]
    if final:
        in_specs += [seq_spec, seq_spec, pl.BlockSpec((1, 2 * HG_VAL_DIM), c2)]
        args += [ofwd, gate, jnp.tile(norm_w, (1, 2))]
    return pl.pallas_call(
        functools.partial(_gla_kernel, chunk=chunk, nl=nl, reverse=reverse, final=final),
        grid=(B // nb, N),
        in_specs=in_specs,
        out_specs=seq_spec,
        out_shape=jax.ShapeDtypeStruct((B, T, W), BF16 if final else F32),
        scratch_shapes=[pltpu.VMEM((nb, HG_HEADS // 2, 2 * HG_VAL_DIM, 2 * HG_KEY_DIM), F32)],
        compiler_params=_cparams(("parallel", "arbitrary")),
        name="gla_bwd" if reverse else "gla_fwd",
    )(*args)


def _attn_kernel(q_ref, k_ref, v_ref, lq1_ref, lk1_ref, lq2_ref, lk2_ref, sub_ref,
                 o_ref, v2_ref, *, lam_init):
    @pl.when(pl.program_id(2) == 0)
    def _():
        v2_ref[:, :DA_VAL_DIM] = v_ref[0]
        v2_ref[:, DA_VAL_DIM:] = jnp.ones((v2_ref.shape[0], DA_VAL_DIM), BF16)

    q = q_ref[0]
    tq = q.shape[0]
    lane = lax.broadcasted_iota(jnp.int32, q.shape, 1)
    zero = jnp.zeros_like(q)
    qq = jnp.concatenate([jnp.where(lane < DA_QK_DIM, q, zero),
                          jnp.where(lane >= DA_QK_DIM, q, zero)], axis=0)
    s = lax.dot_general(qq, k_ref[0], (((1,), (1,)), ((), ())), preferred_element_type=F32)

    def softmax_pv(sc):
        m = jnp.max(sc, axis=-1, keepdims=True)
        p = jnp.exp2(sc - m).astype(BF16)
        ov = jnp.dot(p, v2_ref[...], preferred_element_type=F32)
        return ov[:, :DA_VAL_DIM] / ov[:, DA_VAL_DIM:DA_VAL_DIM + 1]

    rows = tq // ATTN_SPLIT
    o0 = jnp.concatenate([softmax_pv(s[i * rows:(i + 1) * rows]) for i in range(ATTN_SPLIT)], axis=0)
    o1 = jnp.concatenate([softmax_pv(s[tq + i * rows:tq + (i + 1) * rows])
                          for i in range(ATTN_SPLIT)], axis=0)
    lam = (jnp.exp(jnp.sum(lq1_ref[...] * lk1_ref[...], axis=-1, keepdims=True))
           - jnp.exp(jnp.sum(lq2_ref[...] * lk2_ref[...], axis=-1, keepdims=True)) + lam_init)
    o = o0 - lam * o1
    ms = jnp.mean(o * o, axis=-1, keepdims=True)
    o = o * lax.rsqrt(ms + NORM_EPS) * sub_ref[...] * (1.0 - lam_init)
    o_ref[0] = o.astype(o_ref.dtype)


def _attention(q, k, v, lq1, lk1, lq2, lk2, subln_w, layer, tq):
    B, T, W = q.shape
    lam_init = 0.8 - 0.6 * math.exp(-0.3 * layer)
    qblk = pl.BlockSpec((1, tq, DA_VAL_DIM), lambda b, h, i: (b, i, h))
    kblk = pl.BlockSpec((1, T, DA_VAL_DIM), lambda b, h, i: (b, 0, h))
    row128 = pl.BlockSpec((1, V7X_LANES), lambda b, h, i: (0, 0))
    row64 = pl.BlockSpec((1, DA_QK_DIM), lambda b, h, i: (0, 0))
    return pl.pallas_call(
        functools.partial(_attn_kernel, lam_init=lam_init),
        grid=(B, DA_HEADS, T // tq),
        in_specs=[qblk, kblk, kblk, row64, row64, row64, row64, row128],
        out_specs=qblk,
        out_shape=jax.ShapeDtypeStruct((B, T, W), BF16),
        scratch_shapes=[pltpu.VMEM((T, 2 * DA_VAL_DIM), BF16)],
        compiler_params=_cparams(("parallel", "parallel", "arbitrary")),
        name="diff_attn",
    )(q, k, v, lq1.reshape(1, -1), lk1.reshape(1, -1), lq2.reshape(1, -1), lk2.reshape(1, -1),
      subln_w.reshape(1, -1))


def _outproj_kernel(ohg_ref, oda_ref, x_ref, wo_ref, n2_ref, rwh_ref, rwl_ref,
                    h_ref, xn_ref, aff_ref):
    mix = (jnp.dot(ohg_ref[...], wo_ref[:HG_WIDTH, :], preferred_element_type=F32)
           + jnp.dot(oda_ref[...], wo_ref[HG_WIDTH:, :], preferred_element_type=F32))
    h = x_ref[...] + mix
    h_ref[...] = h
    ms = jnp.mean(h * h, axis=-1, keepdims=True)
    xn = h * lax.rsqrt(ms + NORM_EPS) * n2_ref[...]
    xn_hi = xn.astype(BF16)
    xn_ref[...] = xn_hi
    xn_lo = (xn - xn_hi.astype(F32)).astype(BF16)
    logits = (jnp.dot(xn_hi, rwh_ref[...], preferred_element_type=F32)
              + jnp.dot(xn_lo, rwh_ref[...], preferred_element_type=F32)
              + jnp.dot(xn_hi, rwl_ref[...], preferred_element_type=F32))
    lane = lax.broadcasted_iota(jnp.int32, logits.shape, 1)
    logits = jnp.where(lane < N_EXPERTS, logits, -jnp.inf)
    m = jnp.max(logits, axis=-1, keepdims=True)
    e = jnp.exp(logits - m)
    aff_ref[...] = e / jnp.sum(e, axis=-1, keepdims=True)


def _outproj(ohg, oda, x2, w_out_bf, norm2_w, rw_hi, rw_lo, tm):
    n = x2.shape[0]
    row = lambda i: (i, 0)
    const = lambda i: (0, 0)
    return pl.pallas_call(
        _outproj_kernel,
        grid=(n // tm,),
        in_specs=[
            pl.BlockSpec((tm, HG_WIDTH), row),
            pl.BlockSpec((tm, HG_WIDTH), row),
            pl.BlockSpec((tm, D_MODEL), row),
            pl.BlockSpec((D_MODEL, D_MODEL), const),
            pl.BlockSpec((1, D_MODEL), const),
            pl.BlockSpec((D_MODEL, V7X_LANES), const),
            pl.BlockSpec((D_MODEL, V7X_LANES), const),
        ],
        out_specs=[pl.BlockSpec((tm, D_MODEL), row), pl.BlockSpec((tm, D_MODEL), row),
                   pl.BlockSpec((tm, V7X_LANES), row)],
        out_shape=[jax.ShapeDtypeStruct((n, D_MODEL), F32), jax.ShapeDtypeStruct((n, D_MODEL), BF16),
                   jax.ShapeDtypeStruct((n, V7X_LANES), F32)],
        compiler_params=_cparams(("parallel",)),
        name="outproj",
    )(ohg, oda, x2, w_out_bf, norm2_w, rw_hi, rw_lo)


def _select_kernel(aff_ref, tri_ref, low_ref, slot_ref, gslot_ref, toff_ref, *, cap):
    aff = aff_ref[0]
    R = aff.shape[0]
    bits = pltpu.bitcast(aff, jnp.int32)

    def count(mask):
        c = jnp.sum(mask.astype(F32), axis=0, keepdims=True)
        return jnp.sum(c, axis=1, keepdims=True)

    def search(i, t):
        cand = t | (1 << (30 - i))
        return jnp.where(count(bits >= cand) >= cap, cand, t)

    thr = lax.fori_loop(0, 31, search, jnp.zeros((1, 1), jnp.int32))
    gt = bits > thr
    eq = bits == thr
    need = cap - count(gt)

    def prefix(mask):
        x = jnp.where(mask, 1.0, 0.0).astype(BF16)
        pin = jnp.dot(x, tri_ref[...], preferred_element_type=F32)
        off = jnp.dot(low_ref[...], pin.astype(BF16), preferred_element_type=F32)[:, V7X_LANES - 1:]
        return off + pin - x.astype(F32), off, pin

    pe, _, _ = prefix(eq)
    sel = gt | (eq & (pe < need))
    slot, off, pin = prefix(sel)
    slot_ref[0] = jnp.where(sel, slot, -1.0).astype(jnp.int32)
    toff_ref[0] = off.astype(jnp.int32)

    s_row = lax.broadcasted_iota(jnp.int32, (1, cap), 1).astype(F32)
    row_incl = off + pin[:, V7X_LANES - 1:]
    r_of_s = jnp.sum((row_incl <= s_row).astype(F32), axis=0, keepdims=True)
    onehot = lax.broadcasted_iota(jnp.int32, (R, cap), 0).astype(F32) == r_of_s
    base = jnp.sum(jnp.where(onehot, off, 0.0), axis=0, keepdims=True)
    onehot = jnp.where(onehot, 1.0, 0.0).astype(BF16)

    def rows_of_slots(a):
        return lax.dot_general(a, onehot, (((0,), (0,)), ((), ())), preferred_element_type=F32)

    prow = rows_of_slots(pin.astype(BF16))
    l_of_s = jnp.sum((prow <= s_row - base).astype(F32), axis=0, keepdims=True)
    g1 = aff.astype(BF16)
    g2 = (aff - g1.astype(F32)).astype(BF16)
    g3 = (aff - g1.astype(F32) - g2.astype(F32)).astype(BF16)
    grow = rows_of_slots(g1) + rows_of_slots(g2) + rows_of_slots(g3)
    lane_is = lax.broadcasted_iota(jnp.int32, (V7X_LANES, cap), 0).astype(F32) == l_of_s
    gslot_ref[0] = jnp.sum(jnp.where(lane_is, grow, 0.0), axis=0, keepdims=True)


def _select(aff_e, cap):
    E, R, L = aff_e.shape
    tri = jnp.asarray(np.triu(np.ones((L, L), np.float32)), BF16)
    low = jnp.asarray(np.tril(np.ones((R, R), np.float32), -1), BF16)
    blk = pl.BlockSpec((1, R, L), lambda e: (e, 0, 0))
    return pl.pallas_call(
        functools.partial(_select_kernel, cap=cap),
        grid=(E,),
        in_specs=[blk, pl.BlockSpec((L, L), lambda e: (0, 0)), pl.BlockSpec((R, R), lambda e: (0, 0))],
        out_specs=[blk, pl.BlockSpec((1, 1, cap), lambda e: (e, 0, 0)),
                   pl.BlockSpec((1, R, 1), lambda e: (e, 0, 0))],
        out_shape=[jax.ShapeDtypeStruct((E, R, L), jnp.int32),
                   jax.ShapeDtypeStruct((E, 1, cap), F32), jax.ShapeDtypeStruct((E, R, 1), jnp.int32)],
        compiler_params=_cparams(("parallel",)),
        name="select",
    )(aff_e, tri, low)


ROW_ALIGN = 16


def _dispatch_kernel(start_ref, nwin_ref, end_ref, x_ref, slot_ref, xe_hbm,
                     stage_ref, tail_ref, xbuf_ref, sem, xsem, *, cap, win):
    j = pl.program_id(0)
    nt = pl.num_programs(0)
    cur = j % 2
    E = N_EXPERTS
    tj = x_ref.shape[0]

    @pl.when(j == 0)
    def _():
        tail_ref[...] = jnp.zeros_like(tail_ref)
        xbuf_ref[...] = jnp.zeros_like(xbuf_ref)
        pads = [pltpu.make_async_copy(xbuf_ref, xe_hbm.at[e, pl.ds(cap, win)], xsem)
                for e in range(E)]
        for cp in pads:
            cp.start()
        for cp in pads:
            cp.wait()

    def window_write(e, parity, tile):
        dst = xe_hbm.at[e, pl.ds(pl.multiple_of(start_ref[e, tile], ROW_ALIGN), win)]
        return pltpu.make_async_copy(stage_ref.at[parity, e], dst, sem.at[parity, e])

    x = x_ref[...]
    row = lax.broadcasted_iota(jnp.int32, (win, tj), 0)

    def pick(e, first):
        return jnp.where(slot_ref[e:e + 1, :] - first == row, 1.0, 0.0).astype(BF16)

    p_all = jnp.concatenate([pick(e, start_ref[e, j]) for e in range(E)], axis=0)
    slabs = jnp.dot(p_all, x, preferred_element_type=F32)
    for e in range(E):
        slab = slabs[e * win:(e + 1) * win]
        stage_ref[cur, e, :ROW_ALIGN, :] = (slab[:ROW_ALIGN] + tail_ref[e]).astype(BF16)
        stage_ref[cur, e, ROW_ALIGN:, :] = slab[ROW_ALIGN:].astype(BF16)

    @pl.when(j > 0)
    def _():
        for e in range(E):
            window_write(e, 1 - cur, j - 1).wait()

    for e in range(E):
        window_write(e, cur, j).start()

    def boundary_group(buf, filled):
        g = (filled // ROW_ALIGN) * ROW_ALIGN
        gs = pl.multiple_of(jnp.minimum(g, win - ROW_ALIGN), ROW_ALIGN)
        return jnp.where(g < win, buf[pl.ds(gs, ROW_ALIGN), :].astype(F32), 0.0)

    for e in range(E):
        @pl.when(nwin_ref[e, j] == 1)
        def _():
            tail_ref[e] = boundary_group(stage_ref.at[cur, e], end_ref[e, j] - start_ref[e, j])

        def extra(w, carry):
            first = start_ref[e, j] + w * win
            xbuf_ref[...] = jnp.dot(pick(e, first), x, preferred_element_type=F32).astype(BF16)
            dst = xe_hbm.at[e, pl.ds(pl.multiple_of(first, ROW_ALIGN), win)]
            cp = pltpu.make_async_copy(xbuf_ref, dst, xsem)
            cp.start()
            cp.wait()

            @pl.when(w == nwin_ref[e, j] - 1)
            def _():
                tail_ref[e] = boundary_group(xbuf_ref, end_ref[e, j] - first)

            return carry

        lax.fori_loop(1, nwin_ref[e, j], extra, 0)

    @pl.when(j == nt - 1)
    def _():
        for e in range(E):
            window_write(e, cur, j).wait()


def _dispatch(xn2, slot_e, starts, nwin, ends, cap, tj, win):
    n = xn2.shape[0]
    E = N_EXPERTS
    grid_spec = pltpu.PrefetchScalarGridSpec(
        num_scalar_prefetch=3,
        grid=(n // tj,),
        in_specs=[
            pl.BlockSpec((tj, D_MODEL), lambda j, *_: (j, 0)),
            pl.BlockSpec((E, tj), lambda j, *_: (0, j)),
        ],
        out_specs=pl.BlockSpec(memory_space=pl.ANY),
        scratch_shapes=[
            pltpu.VMEM((2, E, win, D_MODEL), BF16),
            pltpu.VMEM((E, ROW_ALIGN, D_MODEL), F32),
            pltpu.VMEM((win, D_MODEL), BF16),
            pltpu.SemaphoreType.DMA((2, E)),
            pltpu.SemaphoreType.DMA,
        ],
    )
    return pl.pallas_call(
        functools.partial(_dispatch_kernel, cap=cap, win=win),
        grid_spec=grid_spec,
        out_shape=jax.ShapeDtypeStruct((E, cap + win, D_MODEL), BF16),
        compiler_params=_cparams(("arbitrary",)),
        name="moe_dispatch",
    )(starts, nwin, ends, xn2, slot_e)


def _ffn_kernel(x_ref, g_ref, w1_ref, w3_ref, w2_ref, o_ref, acc_ref):
    f = pl.program_id(2)
    x = x_ref[0]
    a = jnp.dot(x, w1_ref[0], preferred_element_type=F32)
    b = jnp.dot(x, w3_ref[0], preferred_element_type=F32)
    hid = (a * _sigmoid(a) * b).astype(BF16)
    y = jnp.dot(hid, w2_ref[0], preferred_element_type=F32)

    @pl.when(f == 0)
    def _():
        acc_ref[...] = y

    @pl.when(f > 0)
    def _():
        acc_ref[...] += y

    @pl.when(f == pl.num_programs(2) - 1)
    def _():
        g_col = jnp.broadcast_to(g_ref[0], (V7X_LANES, x.shape[0])).T[:, 0:1]
        o_ref[0] = (acc_ref[...] * g_col).astype(o_ref.dtype)


def _ffn(xe, gslot, w1, w3, w2, cap, tm, tf):
    E = w1.shape[0]
    return pl.pallas_call(
        _ffn_kernel,
        grid=(E, cap // tm, D_FF // tf),
        in_specs=[
            pl.BlockSpec((1, tm, D_MODEL), lambda e, t, f: (e, t, 0)),
            pl.BlockSpec((1, 1, tm), lambda e, t, f: (e, 0, t)),
            pl.BlockSpec((1, D_MODEL, tf), lambda e, t, f: (e, 0, f)),
            pl.BlockSpec((1, D_MODEL, tf), lambda e, t, f: (e, 0, f)),
            pl.BlockSpec((1, tf, D_MODEL), lambda e, t, f: (e, f, 0)),
        ],
        out_specs=pl.BlockSpec((1, tm, D_MODEL), lambda e, t, f: (e, t, 0)),
        out_shape=jax.ShapeDtypeStruct((E, cap, D_MODEL), BF16),
        scratch_shapes=[pltpu.VMEM((tm, D_MODEL), F32)],
        compiler_params=_cparams(("parallel", "parallel", "arbitrary")),
        name="moe_ffn",
    )(xe, gslot, w1, w3, w2)


def _combine_kernel(start_ref, nwin_ref, h_ref, slot_ref, ye_hbm, o_ref,
                    buf_ref, xbuf_ref, sem, xsem, *, cap, win):
    j = pl.program_id(0)
    nt = pl.num_programs(0)
    cur = j % 2
    E = N_EXPERTS
    tj = h_ref.shape[0]

    def window(tile, e, w):
        first = start_ref[e, tile] + w * win
        src = pl.multiple_of(jnp.minimum(first, cap - win), ROW_ALIGN)
        return first, src

    def window_read(tile, e, parity):
        _, src = window(tile, e, 0)
        return pltpu.make_async_copy(ye_hbm.at[e, pl.ds(src, win)],
                                     buf_ref.at[parity, pl.ds(e * win, win)], sem.at[parity, e])

    @pl.when(j == 0)
    def _():
        for e in range(E):
            window_read(0, e, 0).start()

    @pl.when(j + 1 < nt)
    def _():
        for e in range(E):
            window_read(j + 1, e, 1 - cur).start()

    row = lax.broadcasted_iota(jnp.int32, (win, tj), 0)

    def pick(e, first, src):
        s_row = slot_ref[e:e + 1, :]
        hit = (s_row - src == row) & (s_row >= first) & (s_row < first + win)
        return jnp.where(hit, 1.0, 0.0).astype(BF16)

    def scatter_rows(p_t, rows):
        return lax.dot_general(p_t, rows, (((0,), (0,)), ((), ())), preferred_element_type=F32)

    p_all = jnp.concatenate([pick(e, *window(j, e, 0)) for e in range(E)], axis=0)
    for e in range(E):
        window_read(j, e, cur).wait()
    o_ref[...] = h_ref[...] + scatter_rows(p_all, buf_ref[cur])

    for e in range(E):
        def extra(w, carry):
            first, src = window(j, e, w)
            cp = pltpu.make_async_copy(ye_hbm.at[e, pl.ds(src, win)], xbuf_ref, xsem)
            cp.start()
            cp.wait()
            o_ref[...] += scatter_rows(pick(e, first, src), xbuf_ref[...])
            return carry

        lax.fori_loop(1, nwin_ref[e, j], extra, 0)


def _combine(h2, slot_e, ye, starts, nwin, cap, tj, win):
    n = h2.shape[0]
    E = N_EXPERTS
    grid_spec = pltpu.PrefetchScalarGridSpec(
        num_scalar_prefetch=2,
        grid=(n // tj,),
        in_specs=[
            pl.BlockSpec((tj, D_MODEL), lambda j, *_: (j, 0)),
            pl.BlockSpec((E, tj), lambda j, *_: (0, j)),
            pl.BlockSpec(memory_space=pl.ANY),
        ],
        out_specs=pl.BlockSpec((tj, D_MODEL), lambda j, *_: (j, 0)),
        scratch_shapes=[
            pltpu.VMEM((2, E * win, D_MODEL), BF16),
            pltpu.VMEM((win, D_MODEL), BF16),
            pltpu.SemaphoreType.DMA((2, E)),
            pltpu.SemaphoreType.DMA,
        ],
    )
    return pl.pallas_call(
        functools.partial(_combine_kernel, cap=cap, win=win),
        grid_spec=grid_spec,
        out_shape=jax.ShapeDtypeStruct((n, D_MODEL), F32),
        compiler_params=_cparams(("arbitrary",)),
        name="moe_combine",
    )(starts, nwin, h2, slot_e, ye)


TM_INPROJ = 256
TM_OUTPROJ = 256
GLA_CHUNK = 128
GLA_NB = 2
ATTN_TQ = 512
ATTN_SPLIT = 2
FFN_TM = 512
FFN_TF = 1408
MOE_TJ = 512
MOE_WIN = 128


def _moe(h2, xn2, aff, w1, w3, w2):
    n = h2.shape[0]
    E = N_EXPERTS
    cap = CAPACITY_FACTOR * n // E
    R = n // V7X_LANES
    aff_e = aff[:, :E].T.reshape(E, R, V7X_LANES)
    slot, gslot, toff = _select(aff_e, cap)

    tj = min(MOE_TJ, n)
    win = min(MOE_WIN, cap)
    off = toff.reshape(E, R)[:, ::tj // V7X_LANES]
    ends = jnp.concatenate([off[:, 1:], jnp.full((E, 1), cap, jnp.int32)], axis=1)
    starts = (off // ROW_ALIGN) * ROW_ALIGN
    nwin = jnp.maximum(1, (ends - starts + win - 1) // win)
    slot_e = slot.reshape(E, n)

    xe = _dispatch(xn2, slot_e, starts, nwin, ends, cap, tj, win)
    ye = _ffn(xe, gslot, w1, w3, w2, cap, min(FFN_TM, cap), FFN_TF)
    return _combine(h2, slot_e, ye, starts, nwin, cap, tj, win)


def _layer(x, layer, norm1_w, w_in_bf, lb_f, lb_b, hg_norm_w, q_norm_w, k_norm_w,
           lq1, lk1, lq2, lk2, subln_w, w_out_bf, norm2_w, rw_hi, rw_lo, w1, w3, w2):
    B, T, D = x.shape
    n = B * T
    x2 = x.reshape(n, D)
    q, gf, kf, gb, kb, v, gate, aq, ak, av = [
        o.reshape(B, T, GROUP_W)
        for o in _inproj(x2, norm1_w, w_in_bf, lb_f, lb_b, q_norm_w, k_norm_w, T, min(TM_INPROJ, T))]
    chunk = min(GLA_CHUNK, T)
    nb = GLA_NB if B % GLA_NB == 0 else 1
    o_fwd = _gla(q, kf, gf, v, chunk, nb, False)
    o_hg = _gla(q, kb, gb, v, chunk, nb, True, ofwd=o_fwd, gate=gate, norm_w=hg_norm_w)
    o_da = _attention(aq, ak, av, lq1, lk1, lq2, lk2, subln_w, layer, min(ATTN_TQ, T))
    h2, xn2, aff = _outproj(o_hg.reshape(n, GROUP_W), o_da.reshape(n, GROUP_W), x2, w_out_bf,
                            norm2_w, rw_hi, rw_lo, min(TM_OUTPROJ, n))
    return _moe(h2, xn2, aff, w1, w3, w2).reshape(B, T, D)


def kernel(x_prompt, x_sample, norm1_w, w_in, hg_lb_fwd, hg_lb_bwd, hg_norm_w, q_norm_w, k_norm_w,
           lambda_q1, lambda_k1, lambda_q2, lambda_k2, subln_w, w_out, norm2_w, router_w, w1, w3, w2):
    depth = w_in.shape[0]
    assert depth == 1 and hg_lb_fwd.shape[0] == 2
    y_prompt, y_sample = x_prompt, x_sample
    for l in range(depth):
        rw = jnp.pad(router_w[l], ((0, 0), (0, V7X_LANES - N_EXPERTS)))
        rw_hi = rw.astype(BF16)
        rw_lo = (rw - rw_hi.astype(F32)).astype(BF16)
        lp = (l, norm1_w[l:l + 1], w_in[l].astype(BF16), hg_lb_fwd, hg_lb_bwd, hg_norm_w[l:l + 1],
              q_norm_w[l], k_norm_w[l], lambda_q1[l], lambda_k1[l], lambda_q2[l], lambda_k2[l],
              subln_w[l], w_out[l].astype(BF16), norm2_w[l:l + 1], rw_hi, rw_lo,
              w1[l].astype(BF16), w3[l].astype(BF16), w2[l].astype(BF16))
        y_prompt = _layer(y_prompt, *lp)
        y_sample = _layer(y_sample, *lp)
    return (y_prompt, y_sample)
```

```python
import functools
import math

import jax
import jax.numpy as jnp
import numpy as np
from jax import lax
from jax.experimental import pallas as pl
from jax.experimental.pallas import tpu as pltpu

F32 = jnp.float32
BF16 = jnp.bfloat16

D_MODEL = 1024
HG_HEADS = 4
HG_KEY_DIM = 128
HG_VAL_DIM = 128
HG_WIDTH = 512
DA_HEADS = 4
DA_QK_DIM = 64
DA_VAL_DIM = 128
ROT_DIM = 16
ROPE_THETA = 500000.0
N_EXPERTS = 16
CAPACITY_FACTOR = 2
D_FF = 2816
NORM_EPS = 1e-6
GROUP_W = 512
IN_WIDTH = 8 * GROUP_W
LOG2E = 1.4426950408889634

V7X_LANES = 128
VMEM_LIMIT = 56 * 1024 * 1024


def _cparams(sem):
    return pltpu.CompilerParams(dimension_semantics=sem, vmem_limit_bytes=VMEM_LIMIT)


def _sigmoid(x):
    return 1.0 / (1.0 + jnp.exp(-x))


def _rope_tables(T):
    half = ROT_DIM // 2
    inv_freq = 1.0 / (ROPE_THETA ** (jnp.arange(0, ROT_DIM, 2, dtype=F32) / ROT_DIM))
    ang = jnp.arange(T, dtype=F32)[:, None] * inv_freq[None, :]
    cos, sin = jnp.cos(ang), jnp.sin(ang)
    ones = jnp.ones((T, DA_QK_DIM - ROT_DIM), F32)
    zeros = jnp.zeros((T, DA_QK_DIM - ROT_DIM), F32)
    zh = jnp.zeros((T, half), F32)
    c = jnp.concatenate([cos, cos, ones], axis=1)
    sa = jnp.concatenate([-sin, zh, zeros], axis=1)
    sb = jnp.concatenate([zh, sin, zeros], axis=1)
    return tuple(jnp.tile(t, (1, 2)) for t in (c, sa, sb))


def _qk_prep(x, w, c, sa, sb):
    heads = []
    for h in range(DA_HEADS):
        xh = x[:, h * V7X_LANES:(h + 1) * V7X_LANES]
        first = lax.broadcasted_iota(jnp.int32, xh.shape, 1) < DA_QK_DIM
        ss = xh * xh
        s0 = jnp.sum(jnp.where(first, ss, 0.0), axis=-1, keepdims=True)
        s1 = jnp.sum(jnp.where(first, 0.0, ss), axis=-1, keepdims=True)
        r = lax.rsqrt(jnp.where(first, s0, s1) * (1.0 / DA_QK_DIM) + NORM_EPS)
        heads.append(xh * r)
    xn = jnp.concatenate(heads, axis=-1) * w
    lanes = xn.shape[-1]
    return (xn * c + pltpu.roll(xn, lanes - ROT_DIM // 2, 1) * sa
            + pltpu.roll(xn, ROT_DIM // 2, 1) * sb)


def _inproj_kernel(x_ref, n1_ref, w_ref, lbf_ref, lbb_ref, qnw_ref, knw_ref, c_ref, sa_ref, sb_ref,
                   q_ref, gf_ref, kf_ref, gb_ref, kb_ref, v_ref, gate_ref,
                   aq_ref, ak_ref, av_ref):
    x = x_ref[...]
    ms = jnp.mean(x * x, axis=-1, keepdims=True)
    y = (x * lax.rsqrt(ms + NORM_EPS) * n1_ref[...]).astype(BF16)

    def proj(j):
        return jnp.dot(y, w_ref[:, j * GROUP_W:(j + 1) * GROUP_W], preferred_element_type=F32)

    def lower_bound(lb_ref):
        p = lb_ref[...]
        m = jnp.max(p, axis=0, keepdims=True)
        e = jnp.exp(p - m)
        return e[0:1, :] / jnp.sum(e, axis=0, keepdims=True)

    rope = (c_ref[...], sa_ref[...], sb_ref[...])
    aq = _qk_prep(proj(5), qnw_ref[...], *rope)
    aq_ref[...] = (aq * (DA_QK_DIM ** -0.5 * LOG2E)).astype(BF16)
    ak_ref[...] = _qk_prep(proj(6), knw_ref[...], *rope).astype(BF16)

    for j, lb_ref, g_ref, k_ref in ((1, lbf_ref, gf_ref, kf_ref), (2, lbb_ref, gb_ref, kb_ref)):
        lb = lower_bound(lb_ref)
        s = _sigmoid(proj(j))
        f = lb + (1.0 - lb) * s
        g_ref[...] = jnp.log2(f)
        k_ref[...] = ((1.0 - lb) * (1.0 - s)).astype(BF16)

    hq = proj(0)
    q_ref[...] = (hq * _sigmoid(hq) * (HG_KEY_DIM ** -0.5)).astype(BF16)
    hg = proj(4)
    gate_ref[...] = (hg * _sigmoid(hg)).astype(BF16)
    v_ref[...] = proj(3).astype(BF16)
    av_ref[...] = proj(7).astype(BF16)


def _inproj(x2, norm1_w, w_in_bf, lb_f, lb_b, q_norm_w, k_norm_w, T, tm):
    n = x2.shape[0]
    row = lambda i: (i, 0)
    const = lambda i: (0, 0)
    pos = lambda i: (i % (T // tm), 0)
    rope = [jnp.tile(t, (1, DA_HEADS)) for t in _rope_tables(T)]
    qnw = jnp.tile(q_norm_w.reshape(1, DA_QK_DIM), (1, 2 * DA_HEADS))
    knw = jnp.tile(k_norm_w.reshape(1, DA_QK_DIM), (1, 2 * DA_HEADS))
    out_dtypes = [BF16, F32, BF16, F32, BF16, BF16, BF16, BF16, BF16, BF16]
    return pl.pallas_call(
        _inproj_kernel,
        grid=(n // tm,),
        in_specs=[
            pl.BlockSpec((tm, D_MODEL), row),
            pl.BlockSpec((1, D_MODEL), const),
            pl.BlockSpec((D_MODEL, IN_WIDTH), const),
            pl.BlockSpec((2, GROUP_W), const),
            pl.BlockSpec((2, GROUP_W), const),
            pl.BlockSpec((1, GROUP_W), const),
            pl.BlockSpec((1, GROUP_W), const),
            pl.BlockSpec((tm, GROUP_W), pos),
            pl.BlockSpec((tm, GROUP_W), pos),
            pl.BlockSpec((tm, GROUP_W), pos),
        ],
        out_specs=[pl.BlockSpec((tm, GROUP_W), row) for _ in out_dtypes],
        out_shape=[jax.ShapeDtypeStruct((n, GROUP_W), dt) for dt in out_dtypes],
        compiler_params=_cparams(("parallel",)),
        name="inproj",
    )(x2, norm1_w, w_in_bf, lb_f, lb_b, qnw, knw, *rope)


def _gla_constants(chunk, reverse):
    nl = int(math.log2(chunk))
    assert 1 << nl == chunk
    tau = np.arange(chunk)[::-1] if reverse else np.arange(chunk)
    ti = tau[:, None]
    ts = tau[None, :]
    mats = []
    lvl = np.full((chunk, chunk), -1, np.int32)
    lvl[ti == ts] = 0
    for l in range(1, nl + 1):
        L, h = 1 << l, 1 << (l - 1)
        ref = (ti // L) * L + h - 1
        upper = (ti % L) >= h
        m_up = (ts > ref) & (ts <= ti)
        m_lo = (ts > ti) & (ts <= ref)
        mats.append(np.where(upper, m_up, m_lo))
        same = (ti // L) == (ts // L)
        sep = same & ((ti % L) >= h) & ((ts % L) < h)
        lvl[sep] = l
    mats.append(ts <= ti)
    mats.append(ts > ti)
    m = np.concatenate(mats, axis=0).astype(np.float32)
    m2 = np.concatenate([m, m], axis=1)
    upper_rows = np.stack([((tau % (1 << l)) >= (1 << (l - 1))) for l in range(1, nl + 1)], axis=0)
    return m2, lvl, upper_rows.astype(np.int32)[:, :, None], nl


def _gla_kernel(m_ref, lvl_ref, up_ref, q_ref, k_ref, g_ref, v_ref, *rest,
                chunk, nl, reverse, final):
    if final:
        ofwd_ref, gate_ref, nw_ref, o_ref, st_ref = rest
    else:
        o_ref, st_ref = rest
    C = chunk
    nb = q_ref.shape[0]

    @pl.when(pl.program_id(1) == 0)
    def _():
        st_ref[...] = jnp.zeros_like(st_ref)

    lvl = lvl_ref[...]
    level_mask = [lvl == l for l in range(1, nl + 1)]
    upper = [up_ref[l] != 0 for l in range(nl)]
    last = 0 if reverse else C - 1

    parts = []
    for b in range(nb):
        g = g_ref[b]
        g_hi = g.astype(BF16)
        parts += [g_hi, (g - g_hi.astype(F32)).astype(BF16)]
    e_all = []
    for b in range(nb):
        args = jnp.dot(m_ref[...], jnp.concatenate(parts[2 * b:2 * b + 2], axis=0),
                       preferred_element_type=F32)
        e_all.append(jnp.exp2(args))

    W2 = 2 * HG_KEY_DIM
    lane2 = lax.broadcasted_iota(jnp.int32, (C, W2), 1)
    first = lane2 < HG_KEY_DIM
    blk_diag = (lax.broadcasted_iota(jnp.int32, (W2, W2), 0) < HG_KEY_DIM) == (
        lax.broadcasted_iota(jnp.int32, (W2, W2), 1) < HG_KEY_DIM)
    level_mask2 = [jnp.concatenate([m, m], axis=1) for m in level_mask]

    def per_head_rows(x):
        z = jnp.zeros_like(x)
        return jnp.concatenate([jnp.where(first, x, z), jnp.where(first, z, x)], axis=0)

    for b in range(nb):
        outs = []
        for p in range(HG_HEADS // 2):
            sl = slice(p * W2, (p + 1) * W2)
            q = q_ref[b, :, sl].astype(F32)
            k = k_ref[b, :, sl].astype(F32)
            v = v_ref[b, :, sl]
            a = jnp.zeros((C, 2 * C), F32)
            for l in range(1, nl + 1):
                e = e_all[b][(l - 1) * C:l * C, sl]
                x = (jnp.where(upper[l - 1], q, k) * e).astype(BF16)
                a_l = lax.dot_general(x, per_head_rows(x), (((1,), (1,)), ((), ())),
                                      preferred_element_type=F32)
                a = jnp.where(level_mask2[l - 1], a_l, a)
            qk = q * k
            diag = jnp.where(first, jnp.sum(jnp.where(first, qk, 0.0), axis=-1, keepdims=True),
                             jnp.sum(jnp.where(first, 0.0, qk), axis=-1, keepdims=True))
            e_q = e_all[b][nl * C:(nl + 1) * C, sl]
            e_k = e_all[b][(nl + 1) * C:(nl + 2) * C, sl]
            q_in = (q * e_q).astype(BF16)
            k_end = (k * e_k).astype(BF16)
            st = st_ref[b, p]
            o = jnp.dot(a.astype(BF16), per_head_rows(v), preferred_element_type=F32)
            o = o + diag * v.astype(F32)
            o = o + lax.dot_general(q_in, st.astype(BF16), (((1,), (1,)), ((), ())),
                                    preferred_element_type=F32)
            upd = lax.dot_general(v, k_end, (((0,), (0,)), ((), ())), preferred_element_type=F32)
            st_ref[b, p] = st * e_q[last:last + 1, :] + jnp.where(blk_diag, upd, 0.0)
            if final:
                o = o + ofwd_ref[b, :, sl]
                oo = o * o
                ms = jnp.where(first, jnp.sum(jnp.where(first, oo, 0.0), axis=-1, keepdims=True),
                               jnp.sum(jnp.where(first, 0.0, oo), axis=-1, keepdims=True))
                o = o * lax.rsqrt(ms * (1.0 / HG_VAL_DIM) + NORM_EPS) * nw_ref[...]
                o = o * gate_ref[b, :, sl].astype(F32)
            outs.append(o)
        o_ref[b] = jnp.concatenate(outs, axis=-1).astype(o_ref.dtype)


def _gla(q, k, g, v, chunk, nb, reverse, ofwd=None, gate=None, norm_w=None):
    B, T, W = q.shape
    N = T // chunk
    final = reverse
    m2, lvl, up, nl = _gla_constants(chunk, reverse)
    if reverse:
        blk = lambda b, c: (b, N - 1 - c, 0)
    else:
        blk = lambda b, c: (b, c, 0)
    c2 = lambda b, c: (0, 0)
    c3 = lambda b, c: (0, 0, 0)
    seq_spec = pl.BlockSpec((nb, chunk, W), blk)
    in_specs = [
        pl.BlockSpec(m2.shape, c2),
        pl.BlockSpec(lvl.shape, c2),
        pl.BlockSpec(up.shape, c3),
        seq_spec, seq_spec, seq_spec, seq_spec,
    ]
    args = [jnp.asarray(m2, BF16), jnp.asarray(lvl), jnp.asarray(up), q, k, g, v]
    if final:
        in_specs += [seq_spec, seq_spec, pl.BlockSpec((1, 2 * HG_VAL_DIM), c2)]
        args += [ofwd, gate, jnp.tile(norm_w, (1, 2))]
    return pl.pallas_call(
        functools.partial(_gla_kernel, chunk=chunk, nl=nl, reverse=reverse, final=final),
        grid=(B // nb, N),
        in_specs=in_specs,
        out_specs=seq_spec,
        out_shape=jax.ShapeDtypeStruct((B, T, W), BF16 if final else F32),
        scratch_shapes=[pltpu.VMEM((nb, HG_HEADS // 2, 2 * HG_VAL_DIM, 2 * HG_KEY_DIM), F32)],
        compiler_params=_cparams(("parallel", "arbitrary")),
        name="gla_bwd" if reverse else "gla_fwd",
    )(*args)


def _attn_kernel(q_ref, k_ref, v_ref, lq1_ref, lk1_ref, lq2_ref, lk2_ref, sub_ref,
                 o_ref, v2_ref, *, lam_init):
    @pl.when(pl.program_id(2) == 0)
    def _():
        v2_ref[:, :DA_VAL_DIM] = v_ref[0]
        v2_ref[:, DA_VAL_DIM:] = jnp.ones((v2_ref.shape[0], DA_VAL_DIM), BF16)

    q = q_ref[0]
    tq = q.shape[0]
    lane = lax.broadcasted_iota(jnp.int32, q.shape, 1)
    zero = jnp.zeros_like(q)
    qq = jnp.concatenate([jnp.where(lane < DA_QK_DIM, q, zero),
                          jnp.where(lane >= DA_QK_DIM, q, zero)], axis=0)
    s = lax.dot_general(qq, k_ref[0], (((1,), (1,)), ((), ())), preferred_element_type=F32)

    def softmax_pv(sc):
        m = jnp.max(sc, axis=-1, keepdims=True)
        p = jnp.exp2(sc - m).astype(BF16)
        ov = jnp.dot(p, v2_ref[...], preferred_element_type=F32)
        return ov[:, :DA_VAL_DIM] / ov[:, DA_VAL_DIM:DA_VAL_DIM + 1]

    rows = tq // ATTN_SPLIT
    o0 = jnp.concatenate([softmax_pv(s[i * rows:(i + 1) * rows]) for i in range(ATTN_SPLIT)], axis=0)
    o1 = jnp.concatenate([softmax_pv(s[tq + i * rows:tq + (i + 1) * rows])
                          for i in range(ATTN_SPLIT)], axis=0)
    lam = (jnp.exp(jnp.sum(lq1_ref[...] * lk1_ref[...], axis=-1, keepdims=True))
           - jnp.exp(jnp.sum(lq2_ref[...] * lk2_ref[...], axis=-1, keepdims=True)) + lam_init)
    o = o0 - lam * o1
    ms = jnp.mean(o * o, axis=-1, keepdims=True)
    o = o * lax.rsqrt(ms + NORM_EPS) * sub_ref[...] * (1.0 - lam_init)
    o_ref[0] = o.astype(o_ref.dtype)


def _attention(q, k, v, lq1, lk1, lq2, lk2, subln_w, layer, tq):
    B, T, W = q.shape
    lam_init = 0.8 - 0.6 * math.exp(-0.3 * layer)
    qblk = pl.BlockSpec((1, tq, DA_VAL_DIM), lambda b, h, i: (b, i, h))
    kblk = pl.BlockSpec((1, T, DA_VAL_DIM), lambda b, h, i: (b, 0, h))
    row128 = pl.BlockSpec((1, V7X_LANES), lambda b, h, i: (0, 0))
    row64 = pl.BlockSpec((1, DA_QK_DIM), lambda b, h, i: (0, 0))
    return pl.pallas_call(
        functools.partial(_attn_kernel, lam_init=lam_init),
        grid=(B, DA_HEADS, T // tq),
        in_specs=[qblk, kblk, kblk, row64, row64, row64, row64, row128],
        out_specs=qblk,
        out_shape=jax.ShapeDtypeStruct((B, T, W), BF16),
        scratch_shapes=[pltpu.VMEM((T, 2 * DA_VAL_DIM), BF16)],
        compiler_params=_cparams(("parallel", "parallel", "arbitrary")),
        name="diff_attn",
    )(q, k, v, lq1.reshape(1, -1), lk1.reshape(1, -1), lq2.reshape(1, -1), lk2.reshape(1, -1),
      subln_w.reshape(1, -1))


def _outproj_kernel(ohg_ref, oda_ref, x_ref, wo_ref, n2_ref, rw_ref,
                    h_ref, xn_ref, aff_ref):
    mix = (jnp.dot(ohg_ref[...], wo_ref[:HG_WIDTH, :], preferred_element_type=F32)
           + jnp.dot(oda_ref[...], wo_ref[HG_WIDTH:, :], preferred_element_type=F32))
    h = x_ref[...] + mix
    h_ref[...] = h
    ms = jnp.mean(h * h, axis=-1, keepdims=True)
    xn = h * lax.rsqrt(ms + NORM_EPS) * n2_ref[...]
    xn_hi = xn.astype(BF16)
    xn_ref[...] = xn_hi
    xn_lo = (xn - xn_hi.astype(F32)).astype(BF16)
    tm = xn.shape[0]
    parts = jnp.dot(jnp.concatenate([xn_hi, xn_lo], axis=0), rw_ref[...], preferred_element_type=F32)
    both = parts[:tm] + parts[tm:]
    logits = both + pltpu.roll(both, V7X_LANES - N_EXPERTS, 1)
    lane = lax.broadcasted_iota(jnp.int32, logits.shape, 1)
    logits = jnp.where(lane < N_EXPERTS, logits, -jnp.inf)
    m = jnp.max(logits, axis=-1, keepdims=True)
    e = jnp.exp(logits - m)
    aff_ref[...] = e / jnp.sum(e, axis=-1, keepdims=True)


def _outproj(ohg, oda, x2, w_out_bf, norm2_w, rw, tm):
    n = x2.shape[0]
    row = lambda i: (i, 0)
    const = lambda i: (0, 0)
    return pl.pallas_call(
        _outproj_kernel,
        grid=(n // tm,),
        in_specs=[
            pl.BlockSpec((tm, HG_WIDTH), row),
            pl.BlockSpec((tm, HG_WIDTH), row),
            pl.BlockSpec((tm, D_MODEL), row),
            pl.BlockSpec((D_MODEL, D_MODEL), const),
            pl.BlockSpec((1, D_MODEL), const),
            pl.BlockSpec((D_MODEL, V7X_LANES), const),
        ],
        out_specs=[pl.BlockSpec((tm, D_MODEL), row), pl.BlockSpec((tm, D_MODEL), row),
                   pl.BlockSpec((tm, V7X_LANES), row)],
        out_shape=[jax.ShapeDtypeStruct((n, D_MODEL), F32), jax.ShapeDtypeStruct((n, D_MODEL), BF16),
                   jax.ShapeDtypeStruct((n, V7X_LANES), F32)],
        compiler_params=_cparams(("parallel",)),
        name="outproj",
    )(ohg, oda, x2, w_out_bf, norm2_w, rw)


def _select_kernel(aff_ref, tri_ref, low_ref, slot_ref, gslot_ref, toff_ref, *, cap):
    aff = aff_ref[0]
    R = aff.shape[0]
    bits = pltpu.bitcast(aff, jnp.int32)

    def count(mask):
        c = jnp.sum(mask.astype(F32), axis=0, keepdims=True)
        return jnp.sum(c, axis=1, keepdims=True)

    def search(i, t):
        cand = t | (1 << (30 - i))
        return jnp.where(count(bits >= cand) >= cap, cand, t)

    thr = lax.fori_loop(0, 31, search, jnp.zeros((1, 1), jnp.int32))
    gt = bits > thr
    eq = bits == thr
    need = cap - count(gt)

    def prefix(mask):
        x = jnp.where(mask, 1.0, 0.0).astype(BF16)
        pin = jnp.dot(x, tri_ref[...], preferred_element_type=F32)
        off = jnp.dot(low_ref[...], pin.astype(BF16), preferred_element_type=F32)[:, V7X_LANES - 1:]
        return off + pin - x.astype(F32), off, pin

    pe, _, _ = prefix(eq)
    sel = gt | (eq & (pe < need))
    slot, off, pin = prefix(sel)
    slot_ref[0] = jnp.where(sel, slot, -1.0).astype(jnp.int32)
    toff_ref[0] = off.astype(jnp.int32)

    s_row = lax.broadcasted_iota(jnp.int32, (1, cap), 1).astype(F32)
    row_incl = off + pin[:, V7X_LANES - 1:]
    r_of_s = jnp.sum((row_incl <= s_row).astype(F32), axis=0, keepdims=True)
    onehot = lax.broadcasted_iota(jnp.int32, (R, cap), 0).astype(F32) == r_of_s
    base = jnp.sum(jnp.where(onehot, off, 0.0), axis=0, keepdims=True)
    onehot = jnp.where(onehot, 1.0, 0.0).astype(BF16)

    def rows_of_slots(a):
        return lax.dot_general(a, onehot, (((0,), (0,)), ((), ())), preferred_element_type=F32)

    prow = rows_of_slots(pin.astype(BF16))
    l_of_s = jnp.sum((prow <= s_row - base).astype(F32), axis=0, keepdims=True)
    g1 = aff.astype(BF16)
    g2 = (aff - g1.astype(F32)).astype(BF16)
    g3 = (aff - g1.astype(F32) - g2.astype(F32)).astype(BF16)
    grow = rows_of_slots(g1) + rows_of_slots(g2) + rows_of_slots(g3)
    lane_is = lax.broadcasted_iota(jnp.int32, (V7X_LANES, cap), 0).astype(F32) == l_of_s
    gslot_ref[0] = jnp.sum(jnp.where(lane_is, grow, 0.0), axis=0, keepdims=True)


def _select(aff_e, cap):
    E, R, L = aff_e.shape
    tri = jnp.asarray(np.triu(np.ones((L, L), np.float32)), BF16)
    low = jnp.asarray(np.tril(np.ones((R, R), np.float32), -1), BF16)
    blk = pl.BlockSpec((1, R, L), lambda e: (e, 0, 0))
    return pl.pallas_call(
        functools.partial(_select_kernel, cap=cap),
        grid=(E,),
        in_specs=[blk, pl.BlockSpec((L, L), lambda e: (0, 0)), pl.BlockSpec((R, R), lambda e: (0, 0))],
        out_specs=[blk, pl.BlockSpec((1, 1, cap), lambda e: (e, 0, 0)),
                   pl.BlockSpec((1, R, 1), lambda e: (e, 0, 0))],
        out_shape=[jax.ShapeDtypeStruct((E, R, L), jnp.int32),
                   jax.ShapeDtypeStruct((E, 1, cap), F32), jax.ShapeDtypeStruct((E, R, 1), jnp.int32)],
        compiler_params=_cparams(("parallel",)),
        name="select",
    )(aff_e, tri, low)


ROW_ALIGN = 16


def _dispatch_kernel(start_ref, nwin_ref, end_ref, x_ref, slot_ref, xe_hbm,
                     stage_ref, tail_ref, xbuf_ref, sem, xsem, *, cap, win):
    j = pl.program_id(0)
    nt = pl.num_programs(0)
    cur = j % 2
    E = N_EXPERTS
    tj = x_ref.shape[0]

    @pl.when(j == 0)
    def _():
        tail_ref[...] = jnp.zeros_like(tail_ref)
        xbuf_ref[...] = jnp.zeros_like(xbuf_ref)
        pads = [pltpu.make_async_copy(xbuf_ref, xe_hbm.at[e, pl.ds(cap, win)], xsem)
                for e in range(E)]
        for cp in pads:
            cp.start()
        for cp in pads:
            cp.wait()

    def window_write(e, parity, tile):
        dst = xe_hbm.at[e, pl.ds(pl.multiple_of(start_ref[e, tile], ROW_ALIGN), win)]
        return pltpu.make_async_copy(stage_ref.at[parity, e], dst, sem.at[parity, e])

    x = x_ref[...]
    row = lax.broadcasted_iota(jnp.int32, (win, tj), 0)

    def pick(e, first):
        return jnp.where(slot_ref[e:e + 1, :] - first == row, 1.0, 0.0).astype(BF16)

    p_all = jnp.concatenate([pick(e, start_ref[e, j]) for e in range(E)], axis=0)
    slabs = jnp.dot(p_all, x, preferred_element_type=F32)
    for e in range(E):
        slab = slabs[e * win:(e + 1) * win]
        stage_ref[cur, e, :ROW_ALIGN, :] = (slab[:ROW_ALIGN] + tail_ref[e]).astype(BF16)
        stage_ref[cur, e, ROW_ALIGN:, :] = slab[ROW_ALIGN:].astype(BF16)

    @pl.when(j > 0)
    def _():
        for e in range(E):
            window_write(e, 1 - cur, j - 1).wait()

    for e in range(E):
        window_write(e, cur, j).start()

    def boundary_group(buf, filled):
        g = (filled // ROW_ALIGN) * ROW_ALIGN
        gs = pl.multiple_of(jnp.minimum(g, win - ROW_ALIGN), ROW_ALIGN)
        return jnp.where(g < win, buf[pl.ds(gs, ROW_ALIGN), :].astype(F32), 0.0)

    for e in range(E):
        @pl.when(nwin_ref[e, j] == 1)
        def _():
            tail_ref[e] = boundary_group(stage_ref.at[cur, e], end_ref[e, j] - start_ref[e, j])

        def extra(w, carry):
            first = start_ref[e, j] + w * win
            xbuf_ref[...] = jnp.dot(pick(e, first), x, preferred_element_type=F32).astype(BF16)
            dst = xe_hbm.at[e, pl.ds(pl.multiple_of(first, ROW_ALIGN), win)]
            cp = pltpu.make_async_copy(xbuf_ref, dst, xsem)
            cp.start()
            cp.wait()

            @pl.when(w == nwin_ref[e, j] - 1)
            def _():
                tail_ref[e] = boundary_group(xbuf_ref, end_ref[e, j] - first)

            return carry

        lax.fori_loop(1, nwin_ref[e, j], extra, 0)

    @pl.when(j == nt - 1)
    def _():
        for e in range(E):
            window_write(e, cur, j).wait()


def _dispatch(xn2, slot_e, starts, nwin, ends, cap, tj, win):
    n = xn2.shape[0]
    E = N_EXPERTS
    grid_spec = pltpu.PrefetchScalarGridSpec(
        num_scalar_prefetch=3,
        grid=(n // tj,),
        in_specs=[
            pl.BlockSpec((tj, D_MODEL), lambda j, *_: (j, 0)),
            pl.BlockSpec((E, tj), lambda j, *_: (0, j)),
        ],
        out_specs=pl.BlockSpec(memory_space=pl.ANY),
        scratch_shapes=[
            pltpu.VMEM((2, E, win, D_MODEL), BF16),
            pltpu.VMEM((E, ROW_ALIGN, D_MODEL), F32),
            pltpu.VMEM((win, D_MODEL), BF16),
            pltpu.SemaphoreType.DMA((2, E)),
            pltpu.SemaphoreType.DMA,
        ],
    )
    return pl.pallas_call(
        functools.partial(_dispatch_kernel, cap=cap, win=win),
        grid_spec=grid_spec,
        out_shape=jax.ShapeDtypeStruct((E, cap + win, D_MODEL), BF16),
        compiler_params=_cparams(("arbitrary",)),
        name="moe_dispatch",
    )(starts, nwin, ends, xn2, slot_e)


def _ffn_kernel(x_ref, g_ref, w1_ref, w3_ref, w2_ref, o_ref):
    x = x_ref[0]
    a = jnp.dot(x, w1_ref[0], preferred_element_type=F32)
    b = jnp.dot(x, w3_ref[0], preferred_element_type=F32)
    hid = (a * _sigmoid(a) * b).astype(BF16)
    y = jnp.dot(hid, w2_ref[0], preferred_element_type=F32)
    g_col = jnp.broadcast_to(g_ref[0], (V7X_LANES, x.shape[0])).T[:, 0:1]
    o_ref[0] = (y * g_col).astype(o_ref.dtype)


def _ffn(xe, gslot, w1, w3, w2, cap, tm):
    E = w1.shape[0]
    return pl.pallas_call(
        _ffn_kernel,
        grid=(E, cap // tm),
        in_specs=[
            pl.BlockSpec((1, tm, D_MODEL), lambda e, t: (e, t, 0)),
            pl.BlockSpec((1, 1, tm), lambda e, t: (e, 0, t)),
            pl.BlockSpec((1, D_MODEL, D_FF), lambda e, t: (e, 0, 0)),
            pl.BlockSpec((1, D_MODEL, D_FF), lambda e, t: (e, 0, 0)),
            pl.BlockSpec((1, D_FF, D_MODEL), lambda e, t: (e, 0, 0)),
        ],
        out_specs=pl.BlockSpec((1, tm, D_MODEL), lambda e, t: (e, t, 0)),
        out_shape=jax.ShapeDtypeStruct((E, cap, D_MODEL), BF16),
        compiler_params=_cparams(("parallel", "arbitrary")),
        name="moe_ffn",
    )(xe, gslot, w1, w3, w2)


def _combine_kernel(start_ref, nwin_ref, h_ref, slot_ref, ye_hbm, o_ref,
                    buf_ref, xbuf_ref, sem, xsem, *, cap, win):
    j = pl.program_id(0)
    nt = pl.num_programs(0)
    cur = j % 2
    E = N_EXPERTS
    tj = h_ref.shape[0]

    def window(tile, e, w):
        first = start_ref[e, tile] + w * win
        src = pl.multiple_of(jnp.minimum(first, cap - win), ROW_ALIGN)
        return first, src

    def window_read(tile, e, parity):
        _, src = window(tile, e, 0)
        return pltpu.make_async_copy(ye_hbm.at[e, pl.ds(src, win)],
                                     buf_ref.at[parity, pl.ds(e * win, win)], sem.at[parity, e])

    @pl.when(j == 0)
    def _():
        for e in range(E):
            window_read(0, e, 0).start()

    @pl.when(j + 1 < nt)
    def _():
        for e in range(E):
            window_read(j + 1, e, 1 - cur).start()

    row = lax.broadcasted_iota(jnp.int32, (win, tj), 0)

    def pick(e, first, src):
        s_row = slot_ref[e:e + 1, :]
        hit = (s_row - src == row) & (s_row >= first) & (s_row < first + win)
        return jnp.where(hit, 1.0, 0.0).astype(BF16)

    def scatter_rows(p_t, rows):
        return lax.dot_general(p_t, rows, (((0,), (0,)), ((), ())), preferred_element_type=F32)

    p_all = jnp.concatenate([pick(e, *window(j, e, 0)) for e in range(E)], axis=0)
    for e in range(E):
        window_read(j, e, cur).wait()
    o_ref[...] = h_ref[...] + scatter_rows(p_all, buf_ref[cur])

    for e in range(E):
        def extra(w, carry):
            first, src = window(j, e, w)
            cp = pltpu.make_async_copy(ye_hbm.at[e, pl.ds(src, win)], xbuf_ref, xsem)
            cp.start()
            cp.wait()
            o_ref[...] += scatter_rows(pick(e, first, src), xbuf_ref[...])
            return carry

        lax.fori_loop(1, nwin_ref[e, j], extra, 0)


def _combine(h2, slot_e, ye, starts, nwin, cap, tj, win):
    n = h2.shape[0]
    E = N_EXPERTS
    grid_spec = pltpu.PrefetchScalarGridSpec(
        num_scalar_prefetch=2,
        grid=(n // tj,),
        in_specs=[
            pl.BlockSpec((tj, D_MODEL), lambda j, *_: (j, 0)),
            pl.BlockSpec((E, tj), lambda j, *_: (0, j)),
            pl.BlockSpec(memory_space=pl.ANY),
        ],
        out_specs=pl.BlockSpec((tj, D_MODEL), lambda j, *_: (j, 0)),
        scratch_shapes=[
            pltpu.VMEM((2, E * win, D_MODEL), BF16),
            pltpu.VMEM((win, D_MODEL), BF16),
            pltpu.SemaphoreType.DMA((2, E)),
            pltpu.SemaphoreType.DMA,
        ],
    )
    return pl.pallas_call(
        functools.partial(_combine_kernel, cap=cap, win=win),
        grid_spec=grid_spec,
        out_shape=jax.ShapeDtypeStruct((n, D_MODEL), F32),
        compiler_params=_cparams(("arbitrary",)),
        name="moe_combine",
    )(starts, nwin, h2, slot_e, ye)


TM_INPROJ = 256
TM_OUTPROJ = 256
GLA_CHUNK = 128
GLA_NB = 2
ATTN_TQ = 512
ATTN_SPLIT = 2
FFN_TM = 256
DISPATCH_TJ = 512
DISPATCH_WIN = 128
COMBINE_TJ = 256
COMBINE_WIN = 64


def _moe(h2, xn2, aff, w1, w3, w2):
    n = h2.shape[0]
    E = N_EXPERTS
    cap = CAPACITY_FACTOR * n // E
    R = n // V7X_LANES
    aff_e = aff[:, :E].T.reshape(E, R, V7X_LANES)
    slot, gslot, toff = _select(aff_e, cap)

    slot_e = slot.reshape(E, n)

    def windows(tj, win):
        off = toff.reshape(E, R)[:, ::tj // V7X_LANES]
        ends = jnp.concatenate([off[:, 1:], jnp.full((E, 1), cap, jnp.int32)], axis=1)
        starts = (off // ROW_ALIGN) * ROW_ALIGN
        return starts, jnp.maximum(1, (ends - starts + win - 1) // win), ends

    tj, win = min(DISPATCH_TJ, n), min(DISPATCH_WIN, cap)
    starts, nwin, ends = windows(tj, win)
    xe = _dispatch(xn2, slot_e, starts, nwin, ends, cap, tj, win)
    ye = _ffn(xe, gslot, w1, w3, w2, cap, min(FFN_TM, cap))
    tj, win = min(COMBINE_TJ, n), min(COMBINE_WIN, cap)
    starts, nwin, _ = windows(tj, win)
    return _combine(h2, slot_e, ye, starts, nwin, cap, tj, win)


def _layer(x, layer, norm1_w, w_in_bf, lb_f, lb_b, hg_norm_w, q_norm_w, k_norm_w,
           lq1, lk1, lq2, lk2, subln_w, w_out_bf, norm2_w, rw, w1, w3, w2):
    B, T, D = x.shape
    n = B * T
    x2 = x.reshape(n, D)
    q, gf, kf, gb, kb, v, gate, aq, ak, av = [
        o.reshape(B, T, GROUP_W)
        for o in _inproj(x2, norm1_w, w_in_bf, lb_f, lb_b, q_norm_w, k_norm_w, T, min(TM_INPROJ, T))]
    chunk = min(GLA_CHUNK, T)
    nb = GLA_NB if B % GLA_NB == 0 else 1
    o_fwd = _gla(q, kf, gf, v, chunk, nb, False)
    o_hg = _gla(q, kb, gb, v, chunk, nb, True, ofwd=o_fwd, gate=gate, norm_w=hg_norm_w)
    o_da = _attention(aq, ak, av, lq1, lk1, lq2, lk2, subln_w, layer, min(ATTN_TQ, T))
    h2, xn2, aff = _outproj(o_hg.reshape(n, GROUP_W), o_da.reshape(n, GROUP_W), x2, w_out_bf,
                            norm2_w, rw, min(TM_OUTPROJ, n))
    return _moe(h2, xn2, aff, w1, w3, w2).reshape(B, T, D)


def kernel(x_prompt, x_sample, norm1_w, w_in, hg_lb_fwd, hg_lb_bwd, hg_norm_w, q_norm_w, k_norm_w,
           lambda_q1, lambda_k1, lambda_q2, lambda_k2, subln_w, w_out, norm2_w, router_w, w1, w3, w2):
    depth = w_in.shape[0]
    assert depth == 1 and hg_lb_fwd.shape[0] == 2
    y_prompt, y_sample = x_prompt, x_sample
    for l in range(depth):
        rw_hi = router_w[l].astype(BF16)
        rw_lo = (router_w[l] - rw_hi.astype(F32)).astype(BF16)
        rw = jnp.pad(jnp.concatenate([rw_hi, rw_lo], axis=1), ((0, 0), (0, V7X_LANES - 2 * N_EXPERTS)))
        lp = (l, norm1_w[l:l + 1], w_in[l].astype(BF16), hg_lb_fwd, hg_lb_bwd, hg_norm_w[l:l + 1],
              q_norm_w[l], k_norm_w[l], lambda_q1[l], lambda_k1[l], lambda_q2[l], lambda_k2[l],
              subln_w[l], w_out[l].astype(BF16), norm2_w[l:l + 1], rw,
              w1[l].astype(BF16), w3[l].astype(BF16), w2[l].astype(BF16))
        y_prompt = _layer(y_prompt, *lp)
        y_sample = _layer(y_sample, *lp)
    return (y_prompt, y_sample)
```

```python
import functools
import math

import jax
import jax.numpy as jnp
import numpy as np
from jax import lax
from jax.experimental import pallas as pl
from jax.experimental.pallas import tpu as pltpu

F32 = jnp.float32
BF16 = jnp.bfloat16

D_MODEL = 1024
HG_HEADS = 4
HG_KEY_DIM = 128
HG_VAL_DIM = 128
HG_WIDTH = 512
DA_HEADS = 4
DA_QK_DIM = 64
DA_VAL_DIM = 128
ROT_DIM = 16
ROPE_THETA = 500000.0
N_EXPERTS = 16
CAPACITY_FACTOR = 2
D_FF = 2816
NORM_EPS = 1e-6
GROUP_W = 512
IN_WIDTH = 8 * GROUP_W
LOG2E = 1.4426950408889634

V7X_LANES = 128
VMEM_LIMIT = 56 * 1024 * 1024


def _cparams(sem):
    return pltpu.CompilerParams(dimension_semantics=sem, vmem_limit_bytes=VMEM_LIMIT)


def _sigmoid(x):
    return 1.0 / (1.0 + jnp.exp(-x))


def _rope_tables(T):
    half = ROT_DIM // 2
    inv_freq = 1.0 / (ROPE_THETA ** (jnp.arange(0, ROT_DIM, 2, dtype=F32) / ROT_DIM))
    ang = jnp.arange(T, dtype=F32)[:, None] * inv_freq[None, :]
    cos, sin = jnp.cos(ang), jnp.sin(ang)
    ones = jnp.ones((T, DA_QK_DIM - ROT_DIM), F32)
    zeros = jnp.zeros((T, DA_QK_DIM - ROT_DIM), F32)
    zh = jnp.zeros((T, half), F32)
    c = jnp.concatenate([cos, cos, ones], axis=1)
    sa = jnp.concatenate([-sin, zh, zeros], axis=1)
    sb = jnp.concatenate([zh, sin, zeros], axis=1)
    return tuple(jnp.tile(t, (1, 2)) for t in (c, sa, sb))


def _qk_prep(x, w, c, sa, sb):
    heads = []
    for h in range(DA_HEADS):
        xh = x[:, h * V7X_LANES:(h + 1) * V7X_LANES]
        first = lax.broadcasted_iota(jnp.int32, xh.shape, 1) < DA_QK_DIM
        ss = xh * xh
        s0 = jnp.sum(jnp.where(first, ss, 0.0), axis=-1, keepdims=True)
        s1 = jnp.sum(jnp.where(first, 0.0, ss), axis=-1, keepdims=True)
        r = lax.rsqrt(jnp.where(first, s0, s1) * (1.0 / DA_QK_DIM) + NORM_EPS)
        heads.append(xh * r)
    xn = jnp.concatenate(heads, axis=-1) * w
    lanes = xn.shape[-1]
    return (xn * c + pltpu.roll(xn, lanes - ROT_DIM // 2, 1) * sa
            + pltpu.roll(xn, ROT_DIM // 2, 1) * sb)


def _inproj_kernel(x_ref, n1_ref, w_ref, lbf_ref, lbb_ref, qnw_ref, knw_ref, c_ref, sa_ref, sb_ref,
                   q_ref, gf_ref, kf_ref, gb_ref, kb_ref, v_ref, gate_ref,
                   aq_ref, ak_ref, av_ref):
    x = x_ref[...]
    ms = jnp.mean(x * x, axis=-1, keepdims=True)
    y = (x * lax.rsqrt(ms + NORM_EPS) * n1_ref[...]).astype(BF16)

    def proj(j):
        return jnp.dot(y, w_ref[:, j * GROUP_W:(j + 1) * GROUP_W], preferred_element_type=F32)

    def lower_bound(lb_ref):
        p = lb_ref[...]
        m = jnp.max(p, axis=0, keepdims=True)
        e = jnp.exp(p - m)
        return e[0:1, :] / jnp.sum(e, axis=0, keepdims=True)

    rope = (c_ref[...], sa_ref[...], sb_ref[...])
    aq = _qk_prep(proj(5), qnw_ref[...], *rope)
    aq_ref[...] = (aq * (DA_QK_DIM ** -0.5 * LOG2E)).astype(BF16)
    ak_ref[...] = _qk_prep(proj(6), knw_ref[...], *rope).astype(BF16)

    for j, lb_ref, g_ref, k_ref in ((1, lbf_ref, gf_ref, kf_ref), (2, lbb_ref, gb_ref, kb_ref)):
        lb = lower_bound(lb_ref)
        s = _sigmoid(proj(j))
        f = lb + (1.0 - lb) * s
        g_ref[...] = jnp.log2(f)
        k_ref[...] = ((1.0 - lb) * (1.0 - s)).astype(BF16)

    hq = proj(0)
    q_ref[...] = (hq * _sigmoid(hq) * (HG_KEY_DIM ** -0.5)).astype(BF16)
    hg = proj(4)
    gate_ref[...] = (hg * _sigmoid(hg)).astype(BF16)
    v_ref[...] = proj(3).astype(BF16)
    av_ref[...] = proj(7).astype(BF16)


def _inproj(x2, norm1_w, w_in_bf, lb_f, lb_b, q_norm_w, k_norm_w, T, tm):
    n = x2.shape[0]
    row = lambda i: (i, 0)
    const = lambda i: (0, 0)
    pos = lambda i: (i % (T // tm), 0)
    rope = [jnp.tile(t, (1, DA_HEADS)) for t in _rope_tables(T)]
    qnw = jnp.tile(q_norm_w.reshape(1, DA_QK_DIM), (1, 2 * DA_HEADS))
    knw = jnp.tile(k_norm_w.reshape(1, DA_QK_DIM), (1, 2 * DA_HEADS))
    out_dtypes = [BF16, F32, BF16, F32, BF16, BF16, BF16, BF16, BF16, BF16]
    return pl.pallas_call(
        _inproj_kernel,
        grid=(n // tm,),
        in_specs=[
            pl.BlockSpec((tm, D_MODEL), row),
            pl.BlockSpec((1, D_MODEL), const),
            pl.BlockSpec((D_MODEL, IN_WIDTH), const),
            pl.BlockSpec((2, GROUP_W), const),
            pl.BlockSpec((2, GROUP_W), const),
            pl.BlockSpec((1, GROUP_W), const),
            pl.BlockSpec((1, GROUP_W), const),
            pl.BlockSpec((tm, GROUP_W), pos),
            pl.BlockSpec((tm, GROUP_W), pos),
            pl.BlockSpec((tm, GROUP_W), pos),
        ],
        out_specs=[pl.BlockSpec((tm, GROUP_W), row) for _ in out_dtypes],
        out_shape=[jax.ShapeDtypeStruct((n, GROUP_W), dt) for dt in out_dtypes],
        compiler_params=_cparams(("parallel",)),
        name="inproj",
    )(x2, norm1_w, w_in_bf, lb_f, lb_b, qnw, knw, *rope)


def _gla_constants(chunk, reverse):
    nl = int(math.log2(chunk))
    assert 1 << nl == chunk
    tau = np.arange(chunk)[::-1] if reverse else np.arange(chunk)
    ti = tau[:, None]
    ts = tau[None, :]
    mats = []
    lvl = np.full((chunk, chunk), -1, np.int32)
    lvl[ti == ts] = 0
    for l in range(1, nl + 1):
        L, h = 1 << l, 1 << (l - 1)
        ref = (ti // L) * L + h - 1
        upper = (ti % L) >= h
        m_up = (ts > ref) & (ts <= ti)
        m_lo = (ts > ti) & (ts <= ref)
        mats.append(np.where(upper, m_up, m_lo))
        same = (ti // L) == (ts // L)
        sep = same & ((ti % L) >= h) & ((ts % L) < h)
        lvl[sep] = l
    mats.append(ts <= ti)
    mats.append(ts > ti)
    m = np.concatenate(mats, axis=0).astype(np.float32)
    m2 = np.concatenate([m, m], axis=1)
    upper_rows = np.stack([((tau % (1 << l)) >= (1 << (l - 1))) for l in range(1, nl + 1)], axis=0)
    return m2, lvl, upper_rows.astype(np.int32)[:, :, None], nl


def _gla_kernel(m_ref, lvl_ref, up_ref, q_ref, k_ref, g_ref, v_ref, *rest,
                chunk, nl, reverse, final):
    if final:
        ofwd_ref, gate_ref, nw_ref, o_ref, st_ref = rest
    else:
        o_ref, st_ref = rest
    C = chunk
    nb = q_ref.shape[0]

    @pl.when(pl.program_id(1) == 0)
    def _():
        st_ref[...] = jnp.zeros_like(st_ref)

    lvl = lvl_ref[...]
    level_mask = [lvl == l for l in range(1, nl + 1)]
    upper = [up_ref[l] != 0 for l in range(nl)]
    last = 0 if reverse else C - 1

    parts = []
    for b in range(nb):
        g = g_ref[b]
        g_hi = g.astype(BF16)
        parts += [g_hi, (g - g_hi.astype(F32)).astype(BF16)]
    e_all = []
    for b in range(nb):
        args = jnp.dot(m_ref[...], jnp.concatenate(parts[2 * b:2 * b + 2], axis=0),
                       preferred_element_type=F32)
        e_all.append(jnp.exp2(args))

    W2 = 2 * HG_KEY_DIM
    lane2 = lax.broadcasted_iota(jnp.int32, (C, W2), 1)
    first = lane2 < HG_KEY_DIM
    blk_diag = (lax.broadcasted_iota(jnp.int32, (W2, W2), 0) < HG_KEY_DIM) == (
        lax.broadcasted_iota(jnp.int32, (W2, W2), 1) < HG_KEY_DIM)
    level_mask2 = [jnp.concatenate([m, m], axis=1) for m in level_mask]

    def per_head_rows(x):
        z = jnp.zeros_like(x)
        return jnp.concatenate([jnp.where(first, x, z), jnp.where(first, z, x)], axis=0)

    for b in range(nb):
        outs = []
        for p in range(HG_HEADS // 2):
            sl = slice(p * W2, (p + 1) * W2)
            q = q_ref[b, :, sl].astype(F32)
            k = k_ref[b, :, sl].astype(F32)
            v = v_ref[b, :, sl]
            a = jnp.zeros((C, 2 * C), F32)
            for l in range(1, nl + 1):
                e = e_all[b][(l - 1) * C:l * C, sl]
                x = (jnp.where(upper[l - 1], q, k) * e).astype(BF16)
                a_l = lax.dot_general(x, per_head_rows(x), (((1,), (1,)), ((), ())),
                                      preferred_element_type=F32)
                a = jnp.where(level_mask2[l - 1], a_l, a)
            qk = q * k
            diag = jnp.where(first, jnp.sum(jnp.where(first, qk, 0.0), axis=-1, keepdims=True),
                             jnp.sum(jnp.where(first, 0.0, qk), axis=-1, keepdims=True))
            e_q = e_all[b][nl * C:(nl + 1) * C, sl]
            e_k = e_all[b][(nl + 1) * C:(nl + 2) * C, sl]
            q_in = (q * e_q).astype(BF16)
            k_end = (k * e_k).astype(BF16)
            st = st_ref[b, p]
            o = jnp.dot(a.astype(BF16), per_head_rows(v), preferred_element_type=F32)
            o = o + diag * v.astype(F32)
            o = o + lax.dot_general(q_in, st.astype(BF16), (((1,), (1,)), ((), ())),
                                    preferred_element_type=F32)
            upd = lax.dot_general(v, k_end, (((0,), (0,)), ((), ())), preferred_element_type=F32)
            st_ref[b, p] = st * e_q[last:last + 1, :] + jnp.where(blk_diag, upd, 0.0)
            if final:
                o = o + ofwd_ref[b, :, sl]
                oo = o * o
                ms = jnp.where(first, jnp.sum(jnp.where(first, oo, 0.0), axis=-1, keepdims=True),
                               jnp.sum(jnp.where(first, 0.0, oo), axis=-1, keepdims=True))
                o = o * lax.rsqrt(ms * (1.0 / HG_VAL_DIM) + NORM_EPS) * nw_ref[...]
                o = o * gate_ref[b, :, sl].astype(F32)
            outs.append(o)
        o_ref[b] = jnp.concatenate(outs, axis=-1).astype(o_ref.dtype)


def _gla(q, k, g, v, chunk, nb, reverse, ofwd=None, gate=None, norm_w=None):
    B, T, W = q.shape
    N = T // chunk
    final = reverse
    m2, lvl, up, nl = _gla_constants(chunk, reverse)
    if reverse:
        blk = lambda b, c: (b, N - 1 - c, 0)
    else:
        blk = lambda b, c: (b, c, 0)
    c2 = lambda b, c: (0, 0)
    c3 = lambda b, c: (0, 0, 0)
    seq_spec = pl.BlockSpec((nb, chunk, W), blk)
    in_specs = [
        pl.BlockSpec(m2.shape, c2),
        pl.BlockSpec(lvl.shape, c2),
        pl.BlockSpec(up.shape, c3),
        seq_spec, seq_spec, seq_spec, seq_spec,
    ]
    args = [jnp.asarray(m2, BF16), jnp.asarray(lvl), jnp.asarray(up), q, k, g, v]
    if final:
        in_specs += [seq_spec, seq_spec, pl.BlockSpec((1, 2 * HG_VAL_DIM), c2)]
        args += [ofwd, gate, jnp.tile(norm_w, (1, 2))]
    return pl.pallas_call(
        functools.partial(_gla_kernel, chunk=chunk, nl=nl, reverse=reverse, final=final),
        grid=(B // nb, N),
        in_specs=in_specs,
        out_specs=seq_spec,
        out_shape=jax.ShapeDtypeStruct((B, T, W), BF16 if final else F32),
        scratch_shapes=[pltpu.VMEM((nb, HG_HEADS // 2, 2 * HG_VAL_DIM, 2 * HG_KEY_DIM), F32)],
        compiler_params=_cparams(("parallel", "arbitrary")),
        name="gla_bwd" if reverse else "gla_fwd",
    )(*args)


def _attn_kernel(q_ref, k_ref, v_ref, lq1_ref, lk1_ref, lq2_ref, lk2_ref, sub_ref,
                 o_ref, v2_ref, *, lam_init):
    @pl.when(pl.program_id(2) == 0)
    def _():
        v2_ref[:, :DA_VAL_DIM] = v_ref[0]
        v2_ref[:, DA_VAL_DIM:] = jnp.ones((v2_ref.shape[0], DA_VAL_DIM), BF16)

    q = q_ref[0]
    tq = q.shape[0]
    lane = lax.broadcasted_iota(jnp.int32, q.shape, 1)
    zero = jnp.zeros_like(q)
    qq = jnp.concatenate([jnp.where(lane < DA_QK_DIM, q, zero),
                          jnp.where(lane >= DA_QK_DIM, q, zero)], axis=0)
    s = lax.dot_general(qq, k_ref[0], (((1,), (1,)), ((), ())), preferred_element_type=F32)

    def softmax_pv(sc):
        m = jnp.max(sc, axis=-1, keepdims=True)
        p = jnp.exp2(sc - m).astype(BF16)
        ov = jnp.dot(p, v2_ref[...], preferred_element_type=F32)
        return ov[:, :DA_VAL_DIM] / ov[:, DA_VAL_DIM:DA_VAL_DIM + 1]

    rows = tq // ATTN_SPLIT
    o0 = jnp.concatenate([softmax_pv(s[i * rows:(i + 1) * rows]) for i in range(ATTN_SPLIT)], axis=0)
    o1 = jnp.concatenate([softmax_pv(s[tq + i * rows:tq + (i + 1) * rows])
                          for i in range(ATTN_SPLIT)], axis=0)
    lam = (jnp.exp(jnp.sum(lq1_ref[...] * lk1_ref[...], axis=-1, keepdims=True))
           - jnp.exp(jnp.sum(lq2_ref[...] * lk2_ref[...], axis=-1, keepdims=True)) + lam_init)
    o = o0 - lam * o1
    ms = jnp.mean(o * o, axis=-1, keepdims=True)
    o = o * lax.rsqrt(ms + NORM_EPS) * sub_ref[...] * (1.0 - lam_init)
    o_ref[0] = o.astype(o_ref.dtype)


def _attention(q, k, v, lq1, lk1, lq2, lk2, subln_w, layer, tq):
    B, T, W = q.shape
    lam_init = 0.8 - 0.6 * math.exp(-0.3 * layer)
    qblk = pl.BlockSpec((1, tq, DA_VAL_DIM), lambda b, h, i: (b, i, h))
    kblk = pl.BlockSpec((1, T, DA_VAL_DIM), lambda b, h, i: (b, 0, h))
    row128 = pl.BlockSpec((1, V7X_LANES), lambda b, h, i: (0, 0))
    row64 = pl.BlockSpec((1, DA_QK_DIM), lambda b, h, i: (0, 0))
    return pl.pallas_call(
        functools.partial(_attn_kernel, lam_init=lam_init),
        grid=(B, DA_HEADS, T // tq),
        in_specs=[qblk, kblk, kblk, row64, row64, row64, row64, row128],
        out_specs=qblk,
        out_shape=jax.ShapeDtypeStruct((B, T, W), BF16),
        scratch_shapes=[pltpu.VMEM((T, 2 * DA_VAL_DIM), BF16)],
        compiler_params=_cparams(("parallel", "parallel", "arbitrary")),
        name="diff_attn",
    )(q, k, v, lq1.reshape(1, -1), lk1.reshape(1, -1), lq2.reshape(1, -1), lk2.reshape(1, -1),
      subln_w.reshape(1, -1))


def _outproj_kernel(ohg_ref, oda_ref, x_ref, wo_ref, n2_ref, rw_ref,
                    h_ref, xn_ref, aff_ref):
    mix = (jnp.dot(ohg_ref[...], wo_ref[:HG_WIDTH, :], preferred_element_type=F32)
           + jnp.dot(oda_ref[...], wo_ref[HG_WIDTH:, :], preferred_element_type=F32))
    h = x_ref[...] + mix
    h_ref[...] = h
    ms = jnp.mean(h * h, axis=-1, keepdims=True)
    xn = h * lax.rsqrt(ms + NORM_EPS) * n2_ref[...]
    xn_hi = xn.astype(BF16)
    xn_ref[...] = xn_hi
    xn_lo = (xn - xn_hi.astype(F32)).astype(BF16)
    tm = xn.shape[0]
    parts = jnp.dot(jnp.concatenate([xn_hi, xn_lo], axis=0), rw_ref[...], preferred_element_type=F32)
    both = parts[:tm] + parts[tm:]
    logits = both + pltpu.roll(both, V7X_LANES - N_EXPERTS, 1)
    lane = lax.broadcasted_iota(jnp.int32, logits.shape, 1)
    logits = jnp.where(lane < N_EXPERTS, logits, -jnp.inf)
    m = jnp.max(logits, axis=-1, keepdims=True)
    e = jnp.exp(logits - m)
    aff_ref[...] = e / jnp.sum(e, axis=-1, keepdims=True)


def _outproj(ohg, oda, x2, w_out_bf, norm2_w, rw, tm):
    n = x2.shape[0]
    row = lambda i: (i, 0)
    const = lambda i: (0, 0)
    return pl.pallas_call(
        _outproj_kernel,
        grid=(n // tm,),
        in_specs=[
            pl.BlockSpec((tm, HG_WIDTH), row),
            pl.BlockSpec((tm, HG_WIDTH), row),
            pl.BlockSpec((tm, D_MODEL), row),
            pl.BlockSpec((D_MODEL, D_MODEL), const),
            pl.BlockSpec((1, D_MODEL), const),
            pl.BlockSpec((D_MODEL, V7X_LANES), const),
        ],
        out_specs=[pl.BlockSpec((tm, D_MODEL), row), pl.BlockSpec((tm, D_MODEL), row),
                   pl.BlockSpec((tm, V7X_LANES), row)],
        out_shape=[jax.ShapeDtypeStruct((n, D_MODEL), F32), jax.ShapeDtypeStruct((n, D_MODEL), BF16),
                   jax.ShapeDtypeStruct((n, V7X_LANES), F32)],
        compiler_params=_cparams(("parallel",)),
        name="outproj",
    )(ohg, oda, x2, w_out_bf, norm2_w, rw)


def _select_kernel(aff_ref, tri_ref, low_ref, slot_ref, gate_ref, toff_ref, *, cap):
    aff = aff_ref[0]
    bits = pltpu.bitcast(aff, jnp.int32)

    def count(mask):
        c = jnp.sum(mask.astype(F32), axis=0, keepdims=True)
        return jnp.sum(c, axis=1, keepdims=True)

    def search(i, t):
        cand = t | (1 << (30 - i))
        return jnp.where(count(bits >= cand) >= cap, cand, t)

    thr = lax.fori_loop(0, 31, search, jnp.zeros((1, 1), jnp.int32))
    gt = bits > thr
    eq = bits == thr
    need = cap - count(gt)

    def prefix(mask):
        x = jnp.where(mask, 1.0, 0.0).astype(BF16)
        pin = jnp.dot(x, tri_ref[...], preferred_element_type=F32)
        off = jnp.dot(low_ref[...], pin.astype(BF16), preferred_element_type=F32)[:, V7X_LANES - 1:]
        return off + pin - x.astype(F32), off, pin

    pe, _, _ = prefix(eq)
    sel = gt | (eq & (pe < need))
    slot, off, pin = prefix(sel)
    slot_ref[0] = jnp.where(sel, slot, -1.0).astype(jnp.int32)
    gate_ref[0] = jnp.where(sel, aff, 0.0)
    toff_ref[0] = off.astype(jnp.int32)


def _select(aff_e, cap):
    E, R, L = aff_e.shape
    tri = jnp.asarray(np.triu(np.ones((L, L), np.float32)), BF16)
    low = jnp.asarray(np.tril(np.ones((R, R), np.float32), -1), BF16)
    blk = pl.BlockSpec((1, R, L), lambda e: (e, 0, 0))
    return pl.pallas_call(
        functools.partial(_select_kernel, cap=cap),
        grid=(E,),
        in_specs=[blk, pl.BlockSpec((L, L), lambda e: (0, 0)), pl.BlockSpec((R, R), lambda e: (0, 0))],
        out_specs=[blk, blk, pl.BlockSpec((1, R, 1), lambda e: (e, 0, 0))],
        out_shape=[jax.ShapeDtypeStruct((E, R, L), jnp.int32), jax.ShapeDtypeStruct((E, R, L), F32),
                   jax.ShapeDtypeStruct((E, R, 1), jnp.int32)],
        compiler_params=_cparams(("parallel",)),
        name="select",
    )(aff_e, tri, low)


ROW_ALIGN = 16
ROW_W = D_MODEL + V7X_LANES
GATE_TERMS = 3


def _dispatch_kernel(start_ref, nwin_ref, end_ref, x_ref, g_ref, slot_ref, xe_hbm,
                     stage_ref, tail_ref, xbuf_ref, sem, xsem, *, cap, win):
    j = pl.program_id(0)
    nt = pl.num_programs(0)
    cur = j % 2
    E = N_EXPERTS
    tj = x_ref.shape[0]

    @pl.when(j == 0)
    def _():
        tail_ref[...] = jnp.zeros_like(tail_ref)
        xbuf_ref[...] = jnp.zeros_like(xbuf_ref)
        pads = [pltpu.make_async_copy(xbuf_ref, xe_hbm.at[e, pl.ds(cap, win)], xsem)
                for e in range(E)]
        for cp in pads:
            cp.start()
        for cp in pads:
            cp.wait()

    def window_write(e, parity, tile):
        dst = xe_hbm.at[e, pl.ds(pl.multiple_of(start_ref[e, tile], ROW_ALIGN), win)]
        return pltpu.make_async_copy(stage_ref.at[parity, e], dst, sem.at[parity, e])

    x = jnp.concatenate([x_ref[...], g_ref[...]], axis=1)
    row = lax.broadcasted_iota(jnp.int32, (win, tj), 0)

    def pick(e, first):
        return jnp.where(slot_ref[e:e + 1, :] - first == row, 1.0, 0.0).astype(BF16)

    p_all = jnp.concatenate([pick(e, start_ref[e, j]) for e in range(E)], axis=0)
    slabs = jnp.dot(p_all, x, preferred_element_type=F32)
    for e in range(E):
        slab = slabs[e * win:(e + 1) * win]
        stage_ref[cur, e, :ROW_ALIGN, :] = (slab[:ROW_ALIGN] + tail_ref[e]).astype(BF16)
        stage_ref[cur, e, ROW_ALIGN:, :] = slab[ROW_ALIGN:].astype(BF16)

    @pl.when(j > 0)
    def _():
        for e in range(E):
            window_write(e, 1 - cur, j - 1).wait()

    for e in range(E):
        window_write(e, cur, j).start()

    def boundary_group(buf, filled):
        g = (filled // ROW_ALIGN) * ROW_ALIGN
        gs = pl.multiple_of(jnp.minimum(g, win - ROW_ALIGN), ROW_ALIGN)
        return jnp.where(g < win, buf[pl.ds(gs, ROW_ALIGN), :].astype(F32), 0.0)

    for e in range(E):
        @pl.when(nwin_ref[e, j] == 1)
        def _():
            tail_ref[e] = boundary_group(stage_ref.at[cur, e], end_ref[e, j] - start_ref[e, j])

        def extra(w, carry):
            first = start_ref[e, j] + w * win
            xbuf_ref[...] = jnp.dot(pick(e, first), x, preferred_element_type=F32).astype(BF16)
            dst = xe_hbm.at[e, pl.ds(pl.multiple_of(first, ROW_ALIGN), win)]
            cp = pltpu.make_async_copy(xbuf_ref, dst, xsem)
            cp.start()
            cp.wait()

            @pl.when(w == nwin_ref[e, j] - 1)
            def _():
                tail_ref[e] = boundary_group(xbuf_ref, end_ref[e, j] - first)

            return carry

        lax.fori_loop(1, nwin_ref[e, j], extra, 0)

    @pl.when(j == nt - 1)
    def _():
        for e in range(E):
            window_write(e, cur, j).wait()


def _dispatch(xn2, gate_terms, slot_e, starts, nwin, ends, cap, tj, win):
    assert win % ROW_ALIGN == 0 and win > ROW_ALIGN
    n = xn2.shape[0]
    E = N_EXPERTS
    grid_spec = pltpu.PrefetchScalarGridSpec(
        num_scalar_prefetch=3,
        grid=(n // tj,),
        in_specs=[
            pl.BlockSpec((tj, D_MODEL), lambda j, *_: (j, 0)),
            pl.BlockSpec((tj, V7X_LANES), lambda j, *_: (j, 0)),
            pl.BlockSpec((E, tj), lambda j, *_: (0, j)),
        ],
        out_specs=pl.BlockSpec(memory_space=pl.ANY),
        scratch_shapes=[
            pltpu.VMEM((2, E, win, ROW_W), BF16),
            pltpu.VMEM((E, ROW_ALIGN, ROW_W), F32),
            pltpu.VMEM((win, ROW_W), BF16),
            pltpu.SemaphoreType.DMA((2, E)),
            pltpu.SemaphoreType.DMA,
        ],
    )
    return pl.pallas_call(
        functools.partial(_dispatch_kernel, cap=cap, win=win),
        grid_spec=grid_spec,
        out_shape=jax.ShapeDtypeStruct((E, cap + win, ROW_W), BF16),
        compiler_params=_cparams(("arbitrary",)),
        name="moe_dispatch",
    )(starts, nwin, ends, xn2, gate_terms, slot_e)


def _ffn_kernel(x_ref, w1_ref, w3_ref, w2_ref, o_ref):
    x = x_ref[0, :, :D_MODEL]
    a = jnp.dot(x, w1_ref[0], preferred_element_type=F32)
    b = jnp.dot(x, w3_ref[0], preferred_element_type=F32)
    hid = (a * _sigmoid(a) * b).astype(BF16)
    y = jnp.dot(hid, w2_ref[0], preferred_element_type=F32)
    terms = x_ref[0, :, D_MODEL:].astype(F32)
    lane = lax.broadcasted_iota(jnp.int32, terms.shape, 1)
    mine = (lane >= GATE_TERMS * pl.program_id(0)) & (lane < GATE_TERMS * (pl.program_id(0) + 1))
    gate = jnp.sum(jnp.where(mine, terms, 0.0), axis=-1, keepdims=True)
    o_ref[0] = (y * gate).astype(o_ref.dtype)


def _ffn(xe, w1, w3, w2, cap, tm):
    E = w1.shape[0]
    return pl.pallas_call(
        _ffn_kernel,
        grid=(E, cap // tm),
        in_specs=[
            pl.BlockSpec((1, tm, ROW_W), lambda e, t: (e, t, 0)),
            pl.BlockSpec((1, D_MODEL, D_FF), lambda e, t: (e, 0, 0)),
            pl.BlockSpec((1, D_MODEL, D_FF), lambda e, t: (e, 0, 0)),
            pl.BlockSpec((1, D_FF, D_MODEL), lambda e, t: (e, 0, 0)),
        ],
        out_specs=pl.BlockSpec((1, tm, D_MODEL), lambda e, t: (e, t, 0)),
        out_shape=jax.ShapeDtypeStruct((E, cap, D_MODEL), BF16),
        compiler_params=_cparams(("parallel", "arbitrary")),
        name="moe_ffn",
    )(xe, w1, w3, w2)


def _combine_kernel(start_ref, nwin_ref, h_ref, slot_ref, ye_hbm, o_ref,
                    buf_ref, xbuf_ref, sem, xsem, *, cap, win):
    j = pl.program_id(0)
    nt = pl.num_programs(0)
    cur = j % 2
    E = N_EXPERTS
    tj = h_ref.shape[0]

    def window(tile, e, w):
        first = start_ref[e, tile] + w * win
        src = pl.multiple_of(jnp.minimum(first, cap - win), ROW_ALIGN)
        return first, src

    def window_read(tile, e, parity):
        _, src = window(tile, e, 0)
        return pltpu.make_async_copy(ye_hbm.at[e, pl.ds(src, win)],
                                     buf_ref.at[parity, pl.ds(e * win, win)], sem.at[parity, e])

    @pl.when(j == 0)
    def _():
        for e in range(E):
            window_read(0, e, 0).start()

    @pl.when(j + 1 < nt)
    def _():
        for e in range(E):
            window_read(j + 1, e, 1 - cur).start()

    row = lax.broadcasted_iota(jnp.int32, (win, tj), 0)

    def pick(e, first, src):
        s_row = slot_ref[e:e + 1, :]
        hit = (s_row - src == row) & (s_row >= first) & (s_row < first + win)
        return jnp.where(hit, 1.0, 0.0).astype(BF16)

    def scatter_rows(p_t, rows):
        return lax.dot_general(p_t, rows, (((0,), (0,)), ((), ())), preferred_element_type=F32)

    p_all = jnp.concatenate([pick(e, *window(j, e, 0)) for e in range(E)], axis=0)
    for e in range(E):
        window_read(j, e, cur).wait()
    o_ref[...] = h_ref[...] + scatter_rows(p_all, buf_ref[cur])

    for e in range(E):
        def extra(w, carry):
            first, src = window(j, e, w)
            cp = pltpu.make_async_copy(ye_hbm.at[e, pl.ds(src, win)], xbuf_ref, xsem)
            cp.start()
            cp.wait()
            o_ref[...] += scatter_rows(pick(e, first, src), xbuf_ref[...])
            return carry

        lax.fori_loop(1, nwin_ref[e, j], extra, 0)


def _combine(h2, slot_e, ye, starts, nwin, cap, tj, win):
    n = h2.shape[0]
    E = N_EXPERTS
    grid_spec = pltpu.PrefetchScalarGridSpec(
        num_scalar_prefetch=2,
        grid=(n // tj,),
        in_specs=[
            pl.BlockSpec((tj, D_MODEL), lambda j, *_: (j, 0)),
            pl.BlockSpec((E, tj), lambda j, *_: (0, j)),
            pl.BlockSpec(memory_space=pl.ANY),
        ],
        out_specs=pl.BlockSpec((tj, D_MODEL), lambda j, *_: (j, 0)),
        scratch_shapes=[
            pltpu.VMEM((2, E * win, D_MODEL), BF16),
            pltpu.VMEM((win, D_MODEL), BF16),
            pltpu.SemaphoreType.DMA((2, E)),
            pltpu.SemaphoreType.DMA,
        ],
    )
    return pl.pallas_call(
        functools.partial(_combine_kernel, cap=cap, win=win),
        grid_spec=grid_spec,
        out_shape=jax.ShapeDtypeStruct((n, D_MODEL), F32),
        compiler_params=_cparams(("arbitrary",)),
        name="moe_combine",
    )(starts, nwin, h2, slot_e, ye)


TM_INPROJ = 256
TM_OUTPROJ = 256
GLA_CHUNK = 128
GLA_NB = 2
ATTN_TQ = 512
ATTN_SPLIT = 2
FFN_TM = 256
MOE_TJ = 512
MOE_WIN = 128


def _moe(h2, xn2, aff, w1, w3, w2):
    n = h2.shape[0]
    E = N_EXPERTS
    cap = CAPACITY_FACTOR * n // E
    R = n // V7X_LANES
    aff_e = aff[:, :E].T.reshape(E, R, V7X_LANES)
    slot, gate, toff = _select(aff_e, cap)
    slot_e = slot.reshape(E, n)

    g = gate.reshape(E, n).T
    terms = []
    for _ in range(GATE_TERMS):
        t = g.astype(BF16)
        terms.append(t)
        g = g - t.astype(F32)
    gate_terms = jnp.pad(jnp.stack(terms, axis=-1).reshape(n, GATE_TERMS * E),
                         ((0, 0), (0, V7X_LANES - GATE_TERMS * E)))

    tj = min(MOE_TJ, n)
    win = min(MOE_WIN, cap)
    off = toff.reshape(E, R)[:, ::tj // V7X_LANES]
    ends = jnp.concatenate([off[:, 1:], jnp.full((E, 1), cap, jnp.int32)], axis=1)
    starts = (off // ROW_ALIGN) * ROW_ALIGN
    nwin = jnp.maximum(1, (ends - starts + win - 1) // win)

    xe = _dispatch(xn2, gate_terms, slot_e, starts, nwin, ends, cap, tj, win)
    ye = _ffn(xe, w1, w3, w2, cap, min(FFN_TM, cap))
    return _combine(h2, slot_e, ye, starts, nwin, cap, tj, win)


def _layer(x, layer, norm1_w, w_in_bf, lb_f, lb_b, hg_norm_w, q_norm_w, k_norm_w,
           lq1, lk1, lq2, lk2, subln_w, w_out_bf, norm2_w, rw, w1, w3, w2):
    B, T, D = x.shape
    n = B * T
    x2 = x.reshape(n, D)
    q, gf, kf, gb, kb, v, gate, aq, ak, av = [
        o.reshape(B, T, GROUP_W)
        for o in _inproj(x2, norm1_w, w_in_bf, lb_f, lb_b, q_norm_w, k_norm_w, T, min(TM_INPROJ, T))]
    chunk = min(GLA_CHUNK, T)
    nb = GLA_NB if B % GLA_NB == 0 else 1
    o_fwd = _gla(q, kf, gf, v, chunk, nb, False)
    o_hg = _gla(q, kb, gb, v, chunk, nb, True, ofwd=o_fwd, gate=gate, norm_w=hg_norm_w)
    o_da = _attention(aq, ak, av, lq1, lk1, lq2, lk2, subln_w, layer, min(ATTN_TQ, T))
    h2, xn2, aff = _outproj(o_hg.reshape(n, GROUP_W), o_da.reshape(n, GROUP_W), x2, w_out_bf,
                            norm2_w, rw, min(TM_OUTPROJ, n))
    return _moe(h2, xn2, aff, w1, w3, w2).reshape(B, T, D)


def kernel(x_prompt, x_sample, norm1_w, w_in, hg_lb_fwd, hg_lb_bwd, hg_norm_w, q_norm_w, k_norm_w,
           lambda_q1, lambda_k1, lambda_q2, lambda_k2, subln_w, w_out, norm2_w, router_w, w1, w3, w2):
    depth = w_in.shape[0]
    assert depth == 1 and hg_lb_fwd.shape[0] == 2
    y_prompt, y_sample = x_prompt, x_sample
    for l in range(depth):
        rw_hi = router_w[l].astype(BF16)
        rw_lo = (router_w[l] - rw_hi.astype(F32)).astype(BF16)
        rw = jnp.pad(jnp.concatenate([rw_hi, rw_lo], axis=1), ((0, 0), (0, V7X_LANES - 2 * N_EXPERTS)))
        lp = (l, norm1_w[l:l + 1], w_in[l].astype(BF16), hg_lb_fwd, hg_lb_bwd, hg_norm_w[l:l + 1],
              q_norm_w[l], k_norm_w[l], lambda_q1[l], lambda_k1[l], lambda_q2[l], lambda_k2[l],
              subln_w[l], w_out[l].astype(BF16), norm2_w[l:l + 1], rw,
              w1[l].astype(BF16), w3[l].astype(BF16), w2[l].astype(BF16))
        y_prompt = _layer(y_prompt, *lp)
        y_sample = _layer(y_sample, *lp)
    return (y_prompt, y_sample)
```

```python
import functools
import math

import jax
import jax.numpy as jnp
import numpy as np
from jax import lax
from jax.experimental import pallas as pl
from jax.experimental.pallas import tpu as pltpu

F32 = jnp.float32
BF16 = jnp.bfloat16

D_MODEL = 1024
HG_HEADS = 4
HG_KEY_DIM = 128
HG_VAL_DIM = 128
HG_WIDTH = 512
DA_HEADS = 4
DA_QK_DIM = 64
DA_VAL_DIM = 128
ROT_DIM = 16
ROPE_THETA = 500000.0
N_EXPERTS = 16
CAPACITY_FACTOR = 2
D_FF = 2816
NORM_EPS = 1e-6
GROUP_W = 512
IN_WIDTH = 8 * GROUP_W
LOG2E = 1.4426950408889634

V7X_LANES = 128
VMEM_LIMIT = 56 * 1024 * 1024


def _cparams(sem):
    return pltpu.CompilerParams(dimension_semantics=sem, vmem_limit_bytes=VMEM_LIMIT)


def _sigmoid(x):
    return 1.0 / (1.0 + jnp.exp(-x))


def _rope_tables(T):
    half = ROT_DIM // 2
    inv_freq = 1.0 / (ROPE_THETA ** (jnp.arange(0, ROT_DIM, 2, dtype=F32) / ROT_DIM))
    ang = jnp.arange(T, dtype=F32)[:, None] * inv_freq[None, :]
    cos, sin = jnp.cos(ang), jnp.sin(ang)
    ones = jnp.ones((T, DA_QK_DIM - ROT_DIM), F32)
    zeros = jnp.zeros((T, DA_QK_DIM - ROT_DIM), F32)
    zh = jnp.zeros((T, half), F32)
    c = jnp.concatenate([cos, cos, ones], axis=1)
    sa = jnp.concatenate([-sin, zh, zeros], axis=1)
    sb = jnp.concatenate([zh, sin, zeros], axis=1)
    return tuple(jnp.tile(t, (1, 2)) for t in (c, sa, sb))


def _qk_prep(x, w, c, sa, sb):
    heads = []
    for h in range(DA_HEADS):
        xh = x[:, h * V7X_LANES:(h + 1) * V7X_LANES]
        first = lax.broadcasted_iota(jnp.int32, xh.shape, 1) < DA_QK_DIM
        ss = xh * xh
        s0 = jnp.sum(jnp.where(first, ss, 0.0), axis=-1, keepdims=True)
        s1 = jnp.sum(jnp.where(first, 0.0, ss), axis=-1, keepdims=True)
        r = lax.rsqrt(jnp.where(first, s0, s1) * (1.0 / DA_QK_DIM) + NORM_EPS)
        heads.append(xh * r)
    xn = jnp.concatenate(heads, axis=-1) * w
    lanes = xn.shape[-1]
    return (xn * c + pltpu.roll(xn, lanes - ROT_DIM // 2, 1) * sa
            + pltpu.roll(xn, ROT_DIM // 2, 1) * sb)


def _inproj_kernel(x_ref, n1_ref, w_ref, lbf_ref, lbb_ref, qnw_ref, knw_ref, c_ref, sa_ref, sb_ref,
                   q_ref, gf_ref, kf_ref, gb_ref, kb_ref, v_ref, gate_ref,
                   aq_ref, ak_ref, av_ref):
    x = x_ref[...]
    ms = jnp.mean(x * x, axis=-1, keepdims=True)
    y = (x * lax.rsqrt(ms + NORM_EPS) * n1_ref[...]).astype(BF16)

    def proj(j):
        return jnp.dot(y, w_ref[:, j * GROUP_W:(j + 1) * GROUP_W], preferred_element_type=F32)

    def lower_bound(lb_ref):
        p = lb_ref[...]
        m = jnp.max(p, axis=0, keepdims=True)
        e = jnp.exp(p - m)
        return e[0:1, :] / jnp.sum(e, axis=0, keepdims=True)

    rope = (c_ref[...], sa_ref[...], sb_ref[...])
    aq = _qk_prep(proj(5), qnw_ref[...], *rope)
    aq_ref[...] = (aq * (DA_QK_DIM ** -0.5 * LOG2E)).astype(BF16)
    ak_ref[...] = _qk_prep(proj(6), knw_ref[...], *rope).astype(BF16)

    for j, lb_ref, g_ref, k_ref in ((1, lbf_ref, gf_ref, kf_ref), (2, lbb_ref, gb_ref, kb_ref)):
        lb = lower_bound(lb_ref)
        s = _sigmoid(proj(j))
        f = lb + (1.0 - lb) * s
        g_ref[...] = jnp.log2(f)
        k_ref[...] = ((1.0 - lb) * (1.0 - s)).astype(BF16)

    hq = proj(0)
    q_ref[...] = (hq * _sigmoid(hq) * (HG_KEY_DIM ** -0.5)).astype(BF16)
    hg = proj(4)
    gate_ref[...] = (hg * _sigmoid(hg)).astype(BF16)
    v_ref[...] = proj(3).astype(BF16)
    av_ref[...] = proj(7).astype(BF16)


def _inproj(x2, norm1_w, w_in_bf, lb_f, lb_b, q_norm_w, k_norm_w, T, tm):
    n = x2.shape[0]
    row = lambda i: (i, 0)
    const = lambda i: (0, 0)
    pos = lambda i: (i % (T // tm), 0)
    rope = [jnp.tile(t, (1, DA_HEADS)) for t in _rope_tables(T)]
    qnw = jnp.tile(q_norm_w.reshape(1, DA_QK_DIM), (1, 2 * DA_HEADS))
    knw = jnp.tile(k_norm_w.reshape(1, DA_QK_DIM), (1, 2 * DA_HEADS))
    out_dtypes = [BF16, F32, BF16, F32, BF16, BF16, BF16, BF16, BF16, BF16]
    return pl.pallas_call(
        _inproj_kernel,
        grid=(n // tm,),
        in_specs=[
            pl.BlockSpec((tm, D_MODEL), row),
            pl.BlockSpec((1, D_MODEL), const),
            pl.BlockSpec((D_MODEL, IN_WIDTH), const),
            pl.BlockSpec((2, GROUP_W), const),
            pl.BlockSpec((2, GROUP_W), const),
            pl.BlockSpec((1, GROUP_W), const),
            pl.BlockSpec((1, GROUP_W), const),
            pl.BlockSpec((tm, GROUP_W), pos),
            pl.BlockSpec((tm, GROUP_W), pos),
            pl.BlockSpec((tm, GROUP_W), pos),
        ],
        out_specs=[pl.BlockSpec((tm, GROUP_W), row) for _ in out_dtypes],
        out_shape=[jax.ShapeDtypeStruct((n, GROUP_W), dt) for dt in out_dtypes],
        compiler_params=_cparams(("parallel",)),
        name="inproj",
    )(x2, norm1_w, w_in_bf, lb_f, lb_b, qnw, knw, *rope)


def _gla_constants(chunk, reverse):
    nl = int(math.log2(chunk))
    assert 1 << nl == chunk
    tau = np.arange(chunk)[::-1] if reverse else np.arange(chunk)
    ti = tau[:, None]
    ts = tau[None, :]
    mats = []
    lvl = np.full((chunk, chunk), -1, np.int32)
    lvl[ti == ts] = 0
    for l in range(1, nl + 1):
        L, h = 1 << l, 1 << (l - 1)
        ref = (ti // L) * L + h - 1
        upper = (ti % L) >= h
        m_up = (ts > ref) & (ts <= ti)
        m_lo = (ts > ti) & (ts <= ref)
        mats.append(np.where(upper, m_up, m_lo))
        same = (ti // L) == (ts // L)
        sep = same & ((ti % L) >= h) & ((ts % L) < h)
        lvl[sep] = l
    mats.append(ts <= ti)
    mats.append(ts > ti)
    m = np.concatenate(mats, axis=0).astype(np.float32)
    m2 = np.concatenate([m, m], axis=1)
    upper_rows = np.stack([((tau % (1 << l)) >= (1 << (l - 1))) for l in range(1, nl + 1)], axis=0)
    return m2, lvl, upper_rows.astype(np.int32)[:, :, None], nl


def _gla_kernel(m_ref, lvl_ref, up_ref, q_ref, k_ref, g_ref, v_ref, *rest,
                chunk, nl, reverse, final):
    if final:
        ofwd_ref, gate_ref, nw_ref, o_ref, st_ref = rest
    else:
        o_ref, st_ref = rest
    C = chunk
    nb = q_ref.shape[0]

    @pl.when(pl.program_id(1) == 0)
    def _():
        st_ref[...] = jnp.zeros_like(st_ref)

    lvl = lvl_ref[...]
    level_mask = [lvl == l for l in range(1, nl + 1)]
    upper = [up_ref[l] != 0 for l in range(nl)]
    last = 0 if reverse else C - 1

    parts = []
    for b in range(nb):
        g = g_ref[b]
        g_hi = g.astype(BF16)
        parts += [g_hi, (g - g_hi.astype(F32)).astype(BF16)]
    e_all = []
    for b in range(nb):
        args = jnp.dot(m_ref[...], jnp.concatenate(parts[2 * b:2 * b + 2], axis=0),
                       preferred_element_type=F32)
        e_all.append(jnp.exp2(args))

    W2 = 2 * HG_KEY_DIM
    lane2 = lax.broadcasted_iota(jnp.int32, (C, W2), 1)
    first = lane2 < HG_KEY_DIM
    blk_diag = (lax.broadcasted_iota(jnp.int32, (W2, W2), 0) < HG_KEY_DIM) == (
        lax.broadcasted_iota(jnp.int32, (W2, W2), 1) < HG_KEY_DIM)
    level_mask2 = [jnp.concatenate([m, m], axis=1) for m in level_mask]

    def per_head_rows(x):
        z = jnp.zeros_like(x)
        return jnp.concatenate([jnp.where(first, x, z), jnp.where(first, z, x)], axis=0)

    for b in range(nb):
        outs = []
        for p in range(HG_HEADS // 2):
            sl = slice(p * W2, (p + 1) * W2)
            q = q_ref[b, :, sl].astype(F32)
            k = k_ref[b, :, sl].astype(F32)
            v = v_ref[b, :, sl]
            a = jnp.zeros((C, 2 * C), F32)
            for l in range(1, nl + 1):
                e = e_all[b][(l - 1) * C:l * C, sl]
                x = (jnp.where(upper[l - 1], q, k) * e).astype(BF16)
                a_l = lax.dot_general(x, per_head_rows(x), (((1,), (1,)), ((), ())),
                                      preferred_element_type=F32)
                a = jnp.where(level_mask2[l - 1], a_l, a)
            qk = q * k
            diag = jnp.where(first, jnp.sum(jnp.where(first, qk, 0.0), axis=-1, keepdims=True),
                             jnp.sum(jnp.where(first, 0.0, qk), axis=-1, keepdims=True))
            e_q = e_all[b][nl * C:(nl + 1) * C, sl]
            e_k = e_all[b][(nl + 1) * C:(nl + 2) * C, sl]
            q_in = (q * e_q).astype(BF16)
            k_end = (k * e_k).astype(BF16)
            st = st_ref[b, p]
            o = jnp.dot(a.astype(BF16), per_head_rows(v), preferred_element_type=F32)
            o = o + diag * v.astype(F32)
            o = o + lax.dot_general(q_in, st.astype(BF16), (((1,), (1,)), ((), ())),
                                    preferred_element_type=F32)
            upd = lax.dot_general(v, k_end, (((0,), (0,)), ((), ())), preferred_element_type=F32)
            st_ref[b, p] = st * e_q[last:last + 1, :] + jnp.where(blk_diag, upd, 0.0)
            if final:
                o = o + ofwd_ref[b, :, sl].astype(F32)
                oo = o * o
                ms = jnp.where(first, jnp.sum(jnp.where(first, oo, 0.0), axis=-1, keepdims=True),
                               jnp.sum(jnp.where(first, 0.0, oo), axis=-1, keepdims=True))
                o = o * lax.rsqrt(ms * (1.0 / HG_VAL_DIM) + NORM_EPS) * nw_ref[...]
                o = o * gate_ref[b, :, sl].astype(F32)
            outs.append(o)
        o_ref[b] = jnp.concatenate(outs, axis=-1).astype(o_ref.dtype)


def _gla(q, k, g, v, chunk, nb, reverse, ofwd=None, gate=None, norm_w=None):
    B, T, W = q.shape
    N = T // chunk
    final = reverse
    m2, lvl, up, nl = _gla_constants(chunk, reverse)
    if reverse:
        blk = lambda b, c: (b, N - 1 - c, 0)
    else:
        blk = lambda b, c: (b, c, 0)
    c2 = lambda b, c: (0, 0)
    c3 = lambda b, c: (0, 0, 0)
    seq_spec = pl.BlockSpec((nb, chunk, W), blk)
    in_specs = [
        pl.BlockSpec(m2.shape, c2),
        pl.BlockSpec(lvl.shape, c2),
        pl.BlockSpec(up.shape, c3),
        seq_spec, seq_spec, seq_spec, seq_spec,
    ]
    args = [jnp.asarray(m2, BF16), jnp.asarray(lvl), jnp.asarray(up), q, k, g, v]
    if final:
        in_specs += [seq_spec, seq_spec, pl.BlockSpec((1, 2 * HG_VAL_DIM), c2)]
        args += [ofwd, gate, jnp.tile(norm_w, (1, 2))]
    return pl.pallas_call(
        functools.partial(_gla_kernel, chunk=chunk, nl=nl, reverse=reverse, final=final),
        grid=(B // nb, N),
        in_specs=in_specs,
        out_specs=seq_spec,
        out_shape=jax.ShapeDtypeStruct((B, T, W), BF16),
        scratch_shapes=[pltpu.VMEM((nb, HG_HEADS // 2, 2 * HG_VAL_DIM, 2 * HG_KEY_DIM), F32)],
        compiler_params=_cparams(("parallel", "arbitrary")),
        name="gla_bwd" if reverse else "gla_fwd",
    )(*args)


def _attn_kernel(q_ref, k_ref, v_ref, lq1_ref, lk1_ref, lq2_ref, lk2_ref, sub_ref,
                 o_ref, v2_ref, *, lam_init):
    @pl.when(pl.program_id(2) == 0)
    def _():
        v2_ref[:, :DA_VAL_DIM] = v_ref[0]
        v2_ref[:, DA_VAL_DIM:] = jnp.ones((v2_ref.shape[0], DA_VAL_DIM), BF16)

    q = q_ref[0]
    tq = q.shape[0]
    lane = lax.broadcasted_iota(jnp.int32, q.shape, 1)
    zero = jnp.zeros_like(q)
    qq = jnp.concatenate([jnp.where(lane < DA_QK_DIM, q, zero),
                          jnp.where(lane >= DA_QK_DIM, q, zero)], axis=0)
    s = lax.dot_general(qq, k_ref[0], (((1,), (1,)), ((), ())), preferred_element_type=F32)

    def softmax_pv(sc):
        m = jnp.max(sc, axis=-1, keepdims=True)
        p = jnp.exp2(sc - m).astype(BF16)
        ov = jnp.dot(p, v2_ref[...], preferred_element_type=F32)
        return ov[:, :DA_VAL_DIM] / ov[:, DA_VAL_DIM:DA_VAL_DIM + 1]

    rows = tq // ATTN_SPLIT
    o0 = jnp.concatenate([softmax_pv(s[i * rows:(i + 1) * rows]) for i in range(ATTN_SPLIT)], axis=0)
    o1 = jnp.concatenate([softmax_pv(s[tq + i * rows:tq + (i + 1) * rows])
                          for i in range(ATTN_SPLIT)], axis=0)
    lam = (jnp.exp(jnp.sum(lq1_ref[...] * lk1_ref[...], axis=-1, keepdims=True))
           - jnp.exp(jnp.sum(lq2_ref[...] * lk2_ref[...], axis=-1, keepdims=True)) + lam_init)
    o = o0 - lam * o1
    ms = jnp.mean(o * o, axis=-1, keepdims=True)
    o = o * lax.rsqrt(ms + NORM_EPS) * sub_ref[...] * (1.0 - lam_init)
    o_ref[0] = o.astype(o_ref.dtype)


def _attention(q, k, v, lq1, lk1, lq2, lk2, subln_w, layer, tq):
    B, T, W = q.shape
    lam_init = 0.8 - 0.6 * math.exp(-0.3 * layer)
    qblk = pl.BlockSpec((1, tq, DA_VAL_DIM), lambda b, h, i: (b, i, h))
    kblk = pl.BlockSpec((1, T, DA_VAL_DIM), lambda b, h, i: (b, 0, h))
    row128 = pl.BlockSpec((1, V7X_LANES), lambda b, h, i: (0, 0))
    row64 = pl.BlockSpec((1, DA_QK_DIM), lambda b, h, i: (0, 0))
    return pl.pallas_call(
        functools.partial(_attn_kernel, lam_init=lam_init),
        grid=(B, DA_HEADS, T // tq),
        in_specs=[qblk, kblk, kblk, row64, row64, row64, row64, row128],
        out_specs=qblk,
        out_shape=jax.ShapeDtypeStruct((B, T, W), BF16),
        scratch_shapes=[pltpu.VMEM((T, 2 * DA_VAL_DIM), BF16)],
        compiler_params=_cparams(("parallel", "parallel", "arbitrary")),
        name="diff_attn",
    )(q, k, v, lq1.reshape(1, -1), lk1.reshape(1, -1), lq2.reshape(1, -1), lk2.reshape(1, -1),
      subln_w.reshape(1, -1))


def _outproj_kernel(ohg_ref, oda_ref, x_ref, wo_ref, n2_ref, rw_ref,
                    h_ref, xn_ref, aff_ref):
    mix = (jnp.dot(ohg_ref[...], wo_ref[:HG_WIDTH, :], preferred_element_type=F32)
           + jnp.dot(oda_ref[...], wo_ref[HG_WIDTH:, :], preferred_element_type=F32))
    h = x_ref[...] + mix
    h_ref[...] = h
    ms = jnp.mean(h * h, axis=-1, keepdims=True)
    xn = h * lax.rsqrt(ms + NORM_EPS) * n2_ref[...]
    xn_hi = xn.astype(BF16)
    xn_ref[...] = xn_hi
    xn_lo = (xn - xn_hi.astype(F32)).astype(BF16)
    tm = xn.shape[0]
    parts = jnp.dot(jnp.concatenate([xn_hi, xn_lo], axis=0), rw_ref[...], preferred_element_type=F32)
    both = parts[:tm] + parts[tm:]
    logits = both + pltpu.roll(both, V7X_LANES - N_EXPERTS, 1)
    lane = lax.broadcasted_iota(jnp.int32, logits.shape, 1)
    logits = jnp.where(lane < N_EXPERTS, logits, -jnp.inf)
    m = jnp.max(logits, axis=-1, keepdims=True)
    e = jnp.exp(logits - m)
    aff_ref[...] = e / jnp.sum(e, axis=-1, keepdims=True)


def _outproj(ohg, oda, x2, w_out_bf, norm2_w, rw, tm):
    n = x2.shape[0]
    row = lambda i: (i, 0)
    const = lambda i: (0, 0)
    return pl.pallas_call(
        _outproj_kernel,
        grid=(n // tm,),
        in_specs=[
            pl.BlockSpec((tm, HG_WIDTH), row),
            pl.BlockSpec((tm, HG_WIDTH), row),
            pl.BlockSpec((tm, D_MODEL), row),
            pl.BlockSpec((D_MODEL, D_MODEL), const),
            pl.BlockSpec((1, D_MODEL), const),
            pl.BlockSpec((D_MODEL, V7X_LANES), const),
        ],
        out_specs=[pl.BlockSpec((tm, D_MODEL), row), pl.BlockSpec((tm, D_MODEL), row),
                   pl.BlockSpec((tm, V7X_LANES), row)],
        out_shape=[jax.ShapeDtypeStruct((n, D_MODEL), F32), jax.ShapeDtypeStruct((n, D_MODEL), BF16),
                   jax.ShapeDtypeStruct((n, V7X_LANES), F32)],
        compiler_params=_cparams(("parallel",)),
        name="outproj",
    )(ohg, oda, x2, w_out_bf, norm2_w, rw)


def _select_kernel(aff_ref, tri_ref, low_ref, slot_ref, gate_ref, toff_ref, *, cap):
    aff = aff_ref[0]
    bits = pltpu.bitcast(aff, jnp.int32)

    def count(mask):
        c = jnp.sum(mask.astype(F32), axis=0, keepdims=True)
        return jnp.sum(c, axis=1, keepdims=True)

    def search(i, t):
        cand = t | (1 << (30 - i))
        return jnp.where(count(bits >= cand) >= cap, cand, t)

    thr = lax.fori_loop(0, 31, search, jnp.zeros((1, 1), jnp.int32))
    gt = bits > thr
    eq = bits == thr
    need = cap - count(gt)

    def prefix(mask):
        x = jnp.where(mask, 1.0, 0.0).astype(BF16)
        pin = jnp.dot(x, tri_ref[...], preferred_element_type=F32)
        off = jnp.dot(low_ref[...], pin.astype(BF16), preferred_element_type=F32)[:, V7X_LANES - 1:]
        return off + pin - x.astype(F32), off, pin

    pe, _, _ = prefix(eq)
    sel = gt | (eq & (pe < need))
    slot, off, pin = prefix(sel)
    slot_ref[0] = jnp.where(sel, slot, -1.0).astype(jnp.int32)
    gate_ref[0] = jnp.where(sel, aff, 0.0)
    toff_ref[0] = off.astype(jnp.int32)


def _select(aff_e, cap):
    E, R, L = aff_e.shape
    tri = jnp.asarray(np.triu(np.ones((L, L), np.float32)), BF16)
    low = jnp.asarray(np.tril(np.ones((R, R), np.float32), -1), BF16)
    blk = pl.BlockSpec((1, R, L), lambda e: (e, 0, 0))
    return pl.pallas_call(
        functools.partial(_select_kernel, cap=cap),
        grid=(E,),
        in_specs=[blk, pl.BlockSpec((L, L), lambda e: (0, 0)), pl.BlockSpec((R, R), lambda e: (0, 0))],
        out_specs=[blk, blk, pl.BlockSpec((1, R, 1), lambda e: (e, 0, 0))],
        out_shape=[jax.ShapeDtypeStruct((E, R, L), jnp.int32), jax.ShapeDtypeStruct((E, R, L), F32),
                   jax.ShapeDtypeStruct((E, R, 1), jnp.int32)],
        compiler_params=_cparams(("parallel",)),
        name="select",
    )(aff_e, tri, low)


ROW_ALIGN = 16
ROW_W = D_MODEL + V7X_LANES
GATE_TERMS = 3


def _dispatch_kernel(start_ref, nwin_ref, end_ref, x_ref, g_ref, slot_ref, xe_hbm,
                     stage_ref, tail_ref, xbuf_ref, sem, xsem, *, cap, win):
    j = pl.program_id(0)
    nt = pl.num_programs(0)
    cur = j % 2
    E = N_EXPERTS
    tj = x_ref.shape[0]

    @pl.when(j == 0)
    def _():
        tail_ref[...] = jnp.zeros_like(tail_ref)
        xbuf_ref[...] = jnp.zeros_like(xbuf_ref)
        pads = [pltpu.make_async_copy(xbuf_ref, xe_hbm.at[e, pl.ds(cap, win)], xsem)
                for e in range(E)]
        for cp in pads:
            cp.start()
        for cp in pads:
            cp.wait()

    def window_write(e, parity, tile):
        dst = xe_hbm.at[e, pl.ds(pl.multiple_of(start_ref[e, tile], ROW_ALIGN), win)]
        return pltpu.make_async_copy(stage_ref.at[parity, e], dst, sem.at[parity, e])

    x = jnp.concatenate([x_ref[...], g_ref[...]], axis=1)
    row = lax.broadcasted_iota(jnp.int32, (win, tj), 0)

    def pick(e, first):
        return jnp.where(slot_ref[e:e + 1, :] - first == row, 1.0, 0.0).astype(BF16)

    p_all = jnp.concatenate([pick(e, start_ref[e, j]) for e in range(E)], axis=0)
    slabs = jnp.dot(p_all, x, preferred_element_type=F32)
    for e in range(E):
        slab = slabs[e * win:(e + 1) * win]
        stage_ref[cur, e, :ROW_ALIGN, :] = (slab[:ROW_ALIGN] + tail_ref[e]).astype(BF16)
        stage_ref[cur, e, ROW_ALIGN:, :] = slab[ROW_ALIGN:].astype(BF16)

    @pl.when(j > 0)
    def _():
        for e in range(E):
            window_write(e, 1 - cur, j - 1).wait()

    for e in range(E):
        window_write(e, cur, j).start()

    def boundary_group(buf, filled):
        g = (filled // ROW_ALIGN) * ROW_ALIGN
        gs = pl.multiple_of(jnp.minimum(g, win - ROW_ALIGN), ROW_ALIGN)
        return jnp.where(g < win, buf[pl.ds(gs, ROW_ALIGN), :].astype(F32), 0.0)

    for e in range(E):
        @pl.when(nwin_ref[e, j] == 1)
        def _():
            tail_ref[e] = boundary_group(stage_ref.at[cur, e], end_ref[e, j] - start_ref[e, j])

        def extra(w, carry):
            first = start_ref[e, j] + w * win
            xbuf_ref[...] = jnp.dot(pick(e, first), x, preferred_element_type=F32).astype(BF16)
            dst = xe_hbm.at[e, pl.ds(pl.multiple_of(first, ROW_ALIGN), win)]
            cp = pltpu.make_async_copy(xbuf_ref, dst, xsem)
            cp.start()
            cp.wait()

            @pl.when(w == nwin_ref[e, j] - 1)
            def _():
                tail_ref[e] = boundary_group(xbuf_ref, end_ref[e, j] - first)

            return carry

        lax.fori_loop(1, nwin_ref[e, j], extra, 0)

    @pl.when(j == nt - 1)
    def _():
        for e in range(E):
            window_write(e, cur, j).wait()


def _dispatch(xn2, gate_terms, slot_e, starts, nwin, ends, cap, tj, win):
    assert win % ROW_ALIGN == 0 and win > ROW_ALIGN
    n = xn2.shape[0]
    E = N_EXPERTS
    grid_spec = pltpu.PrefetchScalarGridSpec(
        num_scalar_prefetch=3,
        grid=(n // tj,),
        in_specs=[
            pl.BlockSpec((tj, D_MODEL), lambda j, *_: (j, 0)),
            pl.BlockSpec((tj, V7X_LANES), lambda j, *_: (j, 0)),
            pl.BlockSpec((E, tj), lambda j, *_: (0, j)),
        ],
        out_specs=pl.BlockSpec(memory_space=pl.ANY),
        scratch_shapes=[
            pltpu.VMEM((2, E, win, ROW_W), BF16),
            pltpu.VMEM((E, ROW_ALIGN, ROW_W), F32),
            pltpu.VMEM((win, ROW_W), BF16),
            pltpu.SemaphoreType.DMA((2, E)),
            pltpu.SemaphoreType.DMA,
        ],
    )
    return pl.pallas_call(
        functools.partial(_dispatch_kernel, cap=cap, win=win),
        grid_spec=grid_spec,
        out_shape=jax.ShapeDtypeStruct((E, cap + win, ROW_W), BF16),
        compiler_params=_cparams(("arbitrary",)),
        name="moe_dispatch",
    )(starts, nwin, ends, xn2, gate_terms, slot_e)


def _ffn_kernel(x_ref, w1_ref, w3_ref, w2_ref, o_ref):
    rows = x_ref.shape[1] // FFN_PARTS
    for r in range(FFN_PARTS):
        rs = slice(r * rows, (r + 1) * rows)
        x = x_ref[0, rs, :D_MODEL]
        a = jnp.dot(x, w1_ref[0], preferred_element_type=F32)
        b = jnp.dot(x, w3_ref[0], preferred_element_type=F32)
        hid = (a * _sigmoid(a) * b).astype(BF16)
        y = jnp.dot(hid, w2_ref[0], preferred_element_type=F32)
        terms = x_ref[0, rs, D_MODEL:].astype(F32)
        lane = lax.broadcasted_iota(jnp.int32, terms.shape, 1)
        mine = (lane >= GATE_TERMS * pl.program_id(0)) & (lane < GATE_TERMS * (pl.program_id(0) + 1))
        gate = jnp.sum(jnp.where(mine, terms, 0.0), axis=-1, keepdims=True)
        o_ref[0, rs, :] = (y * gate).astype(o_ref.dtype)


def _ffn(xe, w1, w3, w2, cap, tm):
    E = w1.shape[0]
    return pl.pallas_call(
        _ffn_kernel,
        grid=(E, cap // tm),
        in_specs=[
            pl.BlockSpec((1, tm, ROW_W), lambda e, t: (e, t, 0)),
            pl.BlockSpec((1, D_MODEL, D_FF), lambda e, t: (e, 0, 0)),
            pl.BlockSpec((1, D_MODEL, D_FF), lambda e, t: (e, 0, 0)),
            pl.BlockSpec((1, D_FF, D_MODEL), lambda e, t: (e, 0, 0)),
        ],
        out_specs=pl.BlockSpec((1, tm, D_MODEL), lambda e, t: (e, t, 0)),
        out_shape=jax.ShapeDtypeStruct((E, cap, D_MODEL), BF16),
        compiler_params=_cparams(("parallel", "arbitrary")),
        name="moe_ffn",
    )(xe, w1, w3, w2)


def _combine_kernel(start_ref, nwin_ref, h_ref, slot_ref, ye_hbm, o_ref,
                    buf_ref, xbuf_ref, sem, xsem, *, cap, win):
    j = pl.program_id(0)
    nt = pl.num_programs(0)
    cur = j % 2
    E = N_EXPERTS
    tj = h_ref.shape[0]

    def window(tile, e, w):
        first = start_ref[e, tile] + w * win
        src = pl.multiple_of(jnp.minimum(first, cap - win), ROW_ALIGN)
        return first, src

    def window_read(tile, e, parity):
        _, src = window(tile, e, 0)
        return pltpu.make_async_copy(ye_hbm.at[e, pl.ds(src, win)],
                                     buf_ref.at[parity, pl.ds(e * win, win)], sem.at[parity, e])

    @pl.when(j == 0)
    def _():
        for e in range(E):
            window_read(0, e, 0).start()

    @pl.when(j + 1 < nt)
    def _():
        for e in range(E):
            window_read(j + 1, e, 1 - cur).start()

    row = lax.broadcasted_iota(jnp.int32, (win, tj), 0)

    def pick(e, first, src):
        s_row = slot_ref[e:e + 1, :]
        hit = (s_row - src == row) & (s_row >= first) & (s_row < first + win)
        return jnp.where(hit, 1.0, 0.0).astype(BF16)

    def scatter_rows(p_t, rows):
        return lax.dot_general(p_t, rows, (((0,), (0,)), ((), ())), preferred_element_type=F32)

    p_all = jnp.concatenate([pick(e, *window(j, e, 0)) for e in range(E)], axis=0)
    for e in range(E):
        window_read(j, e, cur).wait()
    o_ref[...] = h_ref[...] + scatter_rows(p_all, buf_ref[cur])

    for e in range(E):
        def extra(w, carry):
            first, src = window(j, e, w)
            cp = pltpu.make_async_copy(ye_hbm.at[e, pl.ds(src, win)], xbuf_ref, xsem)
            cp.start()
            cp.wait()
            o_ref[...] += scatter_rows(pick(e, first, src), xbuf_ref[...])
            return carry

        lax.fori_loop(1, nwin_ref[e, j], extra, 0)


def _combine(h2, slot_e, ye, starts, nwin, cap, tj, win):
    n = h2.shape[0]
    E = N_EXPERTS
    grid_spec = pltpu.PrefetchScalarGridSpec(
        num_scalar_prefetch=2,
        grid=(n // tj,),
        in_specs=[
            pl.BlockSpec((tj, D_MODEL), lambda j, *_: (j, 0)),
            pl.BlockSpec((E, tj), lambda j, *_: (0, j)),
            pl.BlockSpec(memory_space=pl.ANY),
        ],
        out_specs=pl.BlockSpec((tj, D_MODEL), lambda j, *_: (j, 0)),
        scratch_shapes=[
            pltpu.VMEM((2, E * win, D_MODEL), BF16),
            pltpu.VMEM((win, D_MODEL), BF16),
            pltpu.SemaphoreType.DMA((2, E)),
            pltpu.SemaphoreType.DMA,
        ],
    )
    return pl.pallas_call(
        functools.partial(_combine_kernel, cap=cap, win=win),
        grid_spec=grid_spec,
        out_shape=jax.ShapeDtypeStruct((n, D_MODEL), F32),
        compiler_params=_cparams(("arbitrary",)),
        name="moe_combine",
    )(starts, nwin, h2, slot_e, ye)


TM_INPROJ = 256
TM_OUTPROJ = 256
GLA_CHUNK = 128
GLA_NB = 2
ATTN_TQ = 512
ATTN_SPLIT = 2
FFN_TM = 512
FFN_PARTS = 2
MOE_TJ = 512
MOE_WIN = 128


def _moe(h2, xn2, aff, w1, w3, w2):
    n = h2.shape[0]
    E = N_EXPERTS
    cap = CAPACITY_FACTOR * n // E
    R = n // V7X_LANES
    aff_e = aff[:, :E].T.reshape(E, R, V7X_LANES)
    slot, gate, toff = _select(aff_e, cap)
    slot_e = slot.reshape(E, n)

    g = gate.reshape(E, n).T
    terms = []
    for _ in range(GATE_TERMS):
        t = g.astype(BF16)
        terms.append(t)
        g = g - t.astype(F32)
    gate_terms = jnp.pad(jnp.stack(terms, axis=-1).reshape(n, GATE_TERMS * E),
                         ((0, 0), (0, V7X_LANES - GATE_TERMS * E)))

    tj = min(MOE_TJ, n)
    win = min(MOE_WIN, cap)
    off = toff.reshape(E, R)[:, ::tj // V7X_LANES]
    ends = jnp.concatenate([off[:, 1:], jnp.full((E, 1), cap, jnp.int32)], axis=1)
    starts = (off // ROW_ALIGN) * ROW_ALIGN
    nwin = jnp.maximum(1, (ends - starts + win - 1) // win)

    xe = _dispatch(xn2, gate_terms, slot_e, starts, nwin, ends, cap, tj, win)
    ye = _ffn(xe, w1, w3, w2, cap, min(FFN_TM, cap))
    return _combine(h2, slot_e, ye, starts, nwin, cap, tj, win)


def _layer(x, layer, norm1_w, w_in_bf, lb_f, lb_b, hg_norm_w, q_norm_w, k_norm_w,
           lq1, lk1, lq2, lk2, subln_w, w_out_bf, norm2_w, rw, w1, w3, w2):
    B, T, D = x.shape
    n = B * T
    x2 = x.reshape(n, D)
    q, gf, kf, gb, kb, v, gate, aq, ak, av = [
        o.reshape(B, T, GROUP_W)
        for o in _inproj(x2, norm1_w, w_in_bf, lb_f, lb_b, q_norm_w, k_norm_w, T, min(TM_INPROJ, T))]
    chunk = min(GLA_CHUNK, T)
    nb = GLA_NB if B % GLA_NB == 0 else 1
    o_fwd = _gla(q, kf, gf, v, chunk, nb, False)
    o_hg = _gla(q, kb, gb, v, chunk, nb, True, ofwd=o_fwd, gate=gate, norm_w=hg_norm_w)
    o_da = _attention(aq, ak, av, lq1, lk1, lq2, lk2, subln_w, layer, min(ATTN_TQ, T))
    h2, xn2, aff = _outproj(o_hg.reshape(n, GROUP_W), o_da.reshape(n, GROUP_W), x2, w_out_bf,
                            norm2_w, rw, min(TM_OUTPROJ, n))
    return _moe(h2, xn2, aff, w1, w3, w2).reshape(B, T, D)


def kernel(x_prompt, x_sample, norm1_w, w_in, hg_lb_fwd, hg_lb_bwd, hg_norm_w, q_norm_w, k_norm_w,
           lambda_q1, lambda_k1, lambda_q2, lambda_k2, subln_w, w_out, norm2_w, router_w, w1, w3, w2):
    depth = w_in.shape[0]
    assert depth == 1 and hg_lb_fwd.shape[0] == 2
    y_prompt, y_sample = x_prompt, x_sample
    for l in range(depth):
        rw_hi = router_w[l].astype(BF16)
        rw_lo = (router_w[l] - rw_hi.astype(F32)).astype(BF16)
        rw = jnp.pad(jnp.concatenate([rw_hi, rw_lo], axis=1), ((0, 0), (0, V7X_LANES - 2 * N_EXPERTS)))
        lp = (l, norm1_w[l:l + 1], w_in[l].astype(BF16), hg_lb_fwd, hg_lb_bwd, hg_norm_w[l:l + 1],
              q_norm_w[l], k_norm_w[l], lambda_q1[l], lambda_k1[l], lambda_q2[l], lambda_k2[l],
              subln_w[l], w_out[l].astype(BF16), norm2_w[l:l + 1], rw,
              w1[l].astype(BF16), w3[l].astype(BF16), w2[l].astype(BF16))
        y_prompt = _layer(y_prompt, *lp)
        y_sample = _layer(y_sample, *lp)
    return (y_prompt, y_sample)
```
